```python
import jax
import jax.numpy as jnp
from jax import lax
import numpy as np

D_MODEL = 2048
BATCH = 4
SEQ = 2048
DEPTH = 1
DEC_BATCH = 32
DEC_SEQ = 1
PAST_LEN = 8192
PAGE_SIZE = 128

HEAD_DIM = 128
SB_HEADS = 8
SB_WIDTH = SB_HEADS * HEAD_DIM
NSA_HEADS = 8
NSA_GROUPS = 2
NSA_HPG = NSA_HEADS // NSA_GROUPS
NSA_WIDTH = NSA_HEADS * HEAD_DIM
NSA_KV_WIDTH = NSA_GROUPS * HEAD_DIM
CMP_BLOCK = 64
SEL_BLOCK = CMP_BLOCK
N_SEL = 16
WINDOW = 512
Q_BLOCK = 128
ROPE_THETA = 10000.0
PEER_HEADS = 8
N_KEYS = 128
N_EXPERTS = N_KEYS * N_KEYS
PEER_TOPK = 16
PEER_DKEY = 256
PEER_HALF = PEER_DKEY // 2
TOK_BLOCK = 128
RMS_EPS = 1e-6
N_ADA = 6
IN_SPLITS = (SB_WIDTH, SB_WIDTH, SB_WIDTH, NSA_WIDTH,
             NSA_KV_WIDTH, NSA_KV_WIDTH, NSA_KV_WIDTH, NSA_KV_WIDTH, NSA_KV_WIDTH, NSA_KV_WIDTH,
             3 * NSA_HEADS, D_MODEL, D_MODEL)
IN_WIDTH = sum(IN_SPLITS)

kernel_name = 'hybrid_stickbreak_nsa_peer_step'


def rmsnorm(x, g):
    xf = x.astype(jnp.float32)
    y = xf * lax.rsqrt(jnp.mean(xf * xf, axis=-1, keepdims=True) + RMS_EPS)
    return (y * g.astype(jnp.float32)).astype(x.dtype)


def adaln(c, w_ada, b_ada):
    mod = jax.nn.silu(c) @ w_ada + b_ada
    return jnp.split(mod[:, None, :], N_ADA, axis=-1)


def modulated_norm(x, g, shift, scale):
    return rmsnorm(x, g) * (1.0 + scale) + shift


def rope(x, pos):
    half = HEAD_DIM // 2
    inv_freq = ROPE_THETA ** (-jnp.arange(half, dtype=jnp.float32) / half)
    ang = pos.astype(jnp.float32)[:, None] * inv_freq[None, :]
    cos = jnp.cos(ang)[None, :, None, :]
    sin = jnp.sin(ang)[None, :, None, :]
    xf = x.astype(jnp.float32)
    x1, x2 = xf[..., :half], xf[..., half:]
    return jnp.concatenate([x1 * cos - x2 * sin, x2 * cos + x1 * sin], axis=-1).astype(x.dtype)


def project(h, w_in, pos):
    B, S, _ = h.shape
    points, acc = [], 0
    for width in IN_SPLITS[:-1]:
        acc += width
        points.append(acc)
    (sb_q, sb_k, sb_v, nq, kc, vc, ks, vs, kw, vw, ng, gate_sb, gate_nsa) = jnp.split(h @ w_in, points, axis=-1)

    def heads(a):
        return a.reshape(B, S, -1, HEAD_DIM)

    return (heads(sb_q), heads(sb_k), heads(sb_v),
            rope(heads(nq), pos),
            rope(heads(kc), pos), heads(vc),
            rope(heads(ks), pos), heads(vs),
            rope(heads(kw), pos), heads(vw),
            ng, gate_sb, gate_nsa)


def stick_breaking(q, k, v, qpos):
    T = k.shape[1]
    kpos = jnp.arange(T, dtype=jnp.int32)
    z = jnp.einsum('bqhd,bkhd->bhqk', q, k).astype(jnp.float32) * (HEAD_DIM ** -0.5)
    mask = (kpos[None, :] < qpos[:, None])[None, None]
    log_rest = jnp.where(mask, jax.nn.log_sigmoid(-z), 0.0)
    tail = lax.cumsum(log_rest, axis=3, reverse=True) - log_rest
    w = jnp.where(mask, jnp.exp(jax.nn.log_sigmoid(z) + tail), 0.0)
    return jnp.einsum('bhqk,bkhd->bqhd', w.astype(v.dtype), v)


def compress(x):
    B, T, G, D = x.shape
    nb = T // CMP_BLOCK
    xb = x[:, :nb * CMP_BLOCK].reshape(B, nb, CMP_BLOCK, G, D).astype(jnp.float32)
    return jnp.mean(xb, axis=2).astype(x.dtype)


def to_blocks(x):
    B, T, G, D = x.shape
    nsb = -(-T // SEL_BLOCK)
    xp = jnp.pad(x, ((0, 0), (0, nsb * SEL_BLOCK - T), (0, 0), (0, 0)))
    return xp.reshape(B, nsb, SEL_BLOCK, G, D).transpose(0, 3, 1, 2, 4)


def nsa_compressed(qg, kb, vb, qpos):
    nb = kb.shape[1]
    z = jnp.einsum('bqghd,bngd->bqghn', qg, kb).astype(jnp.float32) * (HEAD_DIM ** -0.5)
    blk_end = (jnp.arange(nb, dtype=jnp.int32) + 1) * CMP_BLOCK - 1
    vis = (blk_end[None, :] <= qpos[:, None])[None, :, None, None, :]
    zmax = jnp.max(jnp.where(vis, z, -jnp.inf), axis=-1, keepdims=True)
    zmax = jnp.where(jnp.isfinite(zmax), zmax, 0.0)
    e = jnp.where(vis, jnp.exp(z - zmax), 0.0)
    p = e / jnp.maximum(jnp.sum(e, axis=-1, keepdims=True), jnp.finfo(jnp.float32).tiny)
    o = jnp.einsum('bqghn,bngd->bqghd', p.astype(vb.dtype), vb)
    return o, p


def nsa_selected(qg, ksb, vsb, p_cmp, qpos):
    B, Q, G = qg.shape[:3]
    nsb = ksb.shape[2]
    imp = jnp.sum(p_cmp, axis=3)
    imp = jnp.pad(imp, ((0, 0), (0, 0), (0, 0), (0, nsb - imp.shape[-1])))
    blk = jnp.arange(nsb, dtype=jnp.int32)[None, :]
    cur = (qpos // SEL_BLOCK)[:, None]
    imp = jnp.where((blk == cur)[None, :, None, :], jnp.inf,
                    jnp.where((blk > cur)[None, :, None, :], -jnp.inf, imp))
    k_top = min(N_SEL, nsb)
    _, idx = lax.top_k(imp, k_top)
    valid = idx <= (qpos // SEL_BLOCK)[None, :, None, None]
    kpos = idx[..., None] * SEL_BLOCK + jnp.arange(SEL_BLOCK, dtype=jnp.int32)
    kmask = (valid[..., None] & (kpos <= qpos[None, :, None, None, None]))
    kmask = kmask.reshape(B, Q, G, 1, k_top * SEL_BLOCK)
    idx_t = jnp.swapaxes(idx, 1, 2).reshape(B, G, Q * k_top)[:, :, :, None, None]
    kg = jnp.take_along_axis(ksb, idx_t, axis=2).reshape(B, G, Q, k_top * SEL_BLOCK, HEAD_DIM)
    vg = jnp.take_along_axis(vsb, idx_t, axis=2).reshape(B, G, Q, k_top * SEL_BLOCK, HEAD_DIM)
    z = jnp.einsum('bqghd,bgqkd->bqghk', qg, kg).astype(jnp.float32) * (HEAD_DIM ** -0.5)
    p = jax.nn.softmax(jnp.where(kmask, z, -jnp.inf), axis=-1)
    return jnp.einsum('bqghk,bgqkd->bqghd', p.astype(vg.dtype), vg)


def nsa_window(qg, kw, vw, qpos, kpos):
    z = jnp.einsum('bqghd,bkgd->bqghk', qg, kw).astype(jnp.float32) * (HEAD_DIM ** -0.5)
    rel = qpos[:, None] - kpos[None, :]
    m = ((rel >= 0) & (rel < WINDOW) & (kpos[None, :] >= 0))[None, :, None, None, :]
    p = jax.nn.softmax(jnp.where(m, z, -jnp.inf), axis=-1)
    return jnp.einsum('bqghk,bkgd->bqghd', p.astype(vw.dtype), vw)


def attend(sb_q, nsa_q, nsa_gate, qpos, sb_k, sb_v, kb, vb, ksb, vsb, kw, vw, kwpos):
    B, Q = sb_q.shape[:2]
    o_sb = stick_breaking(sb_q, sb_k, sb_v, qpos).reshape(B, Q, SB_WIDTH)
    qg = nsa_q.reshape(B, Q, NSA_GROUPS, NSA_HPG, HEAD_DIM)
    o_cmp, p_cmp = nsa_compressed(qg, kb, vb, qpos)
    o_sel = nsa_selected(qg, ksb, vsb, p_cmp, qpos)
    o_win = nsa_window(qg, kw, vw, qpos, kwpos)
    g = jax.nn.sigmoid(nsa_gate.astype(jnp.float32)).reshape(B, Q, NSA_GROUPS, NSA_HPG, 3, 1).astype(o_cmp.dtype)
    o_nsa = g[..., 0, :] * o_cmp + g[..., 1, :] * o_sel + g[..., 2, :] * o_win
    return o_sb, o_nsa.reshape(B, Q, NSA_WIDTH)


def prompt_mixer(h, w_in):
    B, S, _ = h.shape
    pos = jnp.arange(S, dtype=jnp.int32)
    (sb_q, sb_k, sb_v, nq, kc, vc, ks, vs, kw, vw, ng, gate_sb, gate_nsa) = project(h, w_in, pos)
    kb, vb = compress(kc), compress(vc)
    ksb, vsb = to_blocks(ks), to_blocks(vs)
    pad = ((0, 0), (WINDOW, 0), (0, 0), (0, 0))
    kw_pad, vw_pad = jnp.pad(kw, pad), jnp.pad(vw, pad)
    span = WINDOW + Q_BLOCK

    def body(i):
        s0 = i * Q_BLOCK
        qpos = s0 + jnp.arange(Q_BLOCK, dtype=jnp.int32)

        def sl(a):
            return lax.dynamic_slice_in_dim(a, s0, Q_BLOCK, axis=1)

        kw_b = lax.dynamic_slice_in_dim(kw_pad, s0, span, axis=1)
        vw_b = lax.dynamic_slice_in_dim(vw_pad, s0, span, axis=1)
        kwpos = s0 - WINDOW + jnp.arange(span, dtype=jnp.int32)
        return attend(sl(sb_q), sl(nq), sl(ng), qpos, sb_k, sb_v, kb, vb, ksb, vsb, kw_b, vw_b, kwpos)

    o_sb, o_nsa = lax.map(body, jnp.arange(S // Q_BLOCK, dtype=jnp.int32))
    o_sb = jnp.swapaxes(o_sb, 0, 1).reshape(B, S, SB_WIDTH)
    o_nsa = jnp.swapaxes(o_nsa, 0, 1).reshape(B, S, NSA_WIDTH)
    win_len = min(WINDOW, S)
    new_sb = jnp.stack([sb_k, sb_v], axis=2)
    new_nsa = jnp.stack([kc, vc, ks, vs], axis=2)
    new_win = jnp.stack([kw, vw], axis=2)[:, S - win_len:]
    return o_sb, o_nsa, gate_sb, gate_nsa, new_sb, new_nsa, new_win


def sample_mixer(h, w_in, cache_sb, cache_nsa, cache_win, page_table, layer):
    B, S, _ = h.shape
    past = page_table.shape[1] * cache_sb.shape[2]
    pos = past + jnp.arange(S, dtype=jnp.int32)
    (sb_q, sb_k, sb_v, nq, kc, vc, ks, vs, kw, vw, ng, gate_sb, gate_nsa) = project(h, w_in, pos)
    sb_past = cache_sb[layer, page_table].reshape((B, past) + cache_sb.shape[3:])
    nsa_past = cache_nsa[layer, page_table].reshape((B, past) + cache_nsa.shape[3:])
    win_past = cache_win[layer]

    def cat(a, b):
        return jnp.concatenate([a, b], axis=1)

    sb_k_all, sb_v_all = cat(sb_past[:, :, 0], sb_k), cat(sb_past[:, :, 1], sb_v)
    kc_all, vc_all = cat(nsa_past[:, :, 0], kc), cat(nsa_past[:, :, 1], vc)
    ks_all, vs_all = cat(nsa_past[:, :, 2], ks), cat(nsa_past[:, :, 3], vs)
    win_len = win_past.shape[1]
    kw_all, vw_all = cat(win_past[:, :, 0], kw), cat(win_past[:, :, 1], vw)
    kwpos = past - win_len + jnp.arange(win_len + S, dtype=jnp.int32)
    o_sb, o_nsa = attend(sb_q, nq, ng, pos, sb_k_all, sb_v_all,
                         compress(kc_all), compress(vc_all), to_blocks(ks_all), to_blocks(vs_all),
                         kw_all, vw_all, kwpos)
    new_sb = jnp.stack([sb_k, sb_v], axis=2)
    new_nsa = jnp.stack([kc, vc, ks, vs], axis=2)
    new_win = jnp.stack([kw_all, vw_all], axis=2)[:, S:]
    return o_sb, o_nsa, gate_sb, gate_nsa, new_sb, new_nsa, new_win


def merge_branches(o_sb, o_nsa, gate_sb, gate_nsa, w_up_sb, w_up_nsa, w_o):
    merged = jax.nn.sigmoid(gate_sb) * (o_sb @ w_up_sb) + jax.nn.sigmoid(gate_nsa) * (o_nsa @ w_up_nsa)
    return merged @ w_o


def peer(h, wq, sub_keys, u_tab, v_tab):
    n, d = h.shape
    tb = min(TOK_BLOCK, n)
    hp = jnp.pad(h, ((0, (-n) % tb), (0, 0))).reshape(-1, tb, d)

    def block(hb):
        q = (hb @ wq).reshape(tb, PEER_HEADS, 2, PEER_HALF)
        s = jnp.einsum('thcd,hcnd->thcn', q, sub_keys).astype(jnp.float32)
        top_s, top_i = lax.top_k(s, PEER_TOPK)
        cand_s = (top_s[:, :, 0, :, None] + top_s[:, :, 1, None, :]).reshape(tb, PEER_HEADS, PEER_TOPK * PEER_TOPK)
        cand_i = (top_i[:, :, 0, :, None] * N_KEYS + top_i[:, :, 1, None, :]).reshape(tb, PEER_HEADS, PEER_TOPK * PEER_TOPK)
        fin_s, fin_pos = lax.top_k(cand_s, PEER_TOPK)
        e_idx = jnp.take_along_axis(cand_i, fin_pos, axis=-1)
        g = jax.nn.softmax(fin_s, axis=-1)
        act = jax.nn.gelu(jnp.einsum('thkd,td->thk', u_tab[e_idx], hb).astype(jnp.float32), approximate=False)
        return jnp.einsum('thk,thkd->td', (g * act).astype(hb.dtype), v_tab[e_idx])

    return lax.map(block, hp).reshape(-1, d)[:n]


def peer_sublayer(h, wq, sub_keys, u_tab, v_tab):
    B, S, D = h.shape
    return peer(h.reshape(B * S, D), wq, sub_keys, u_tab, v_tab).reshape(B, S, D)


def setup_inputs(seed: int = 0) -> dict:
    key = jax.random.key(seed)
    ks = jax.random.split(key, 24)
    n_pages = PAST_LEN // PAGE_SIZE
    n_pool = (DEC_BATCH * n_pages * 5) // 4
    win_buf = min(WINDOW, PAST_LEN)
    f32 = jnp.float32

    def nrm(k, shape, s=1.0):
        return s * jax.random.normal(k, shape, f32)

    page_table = jax.random.permutation(ks[5], n_pool)[:DEC_BATCH * n_pages].reshape(DEC_BATCH, n_pages).astype(jnp.int32)
    return {
        'x_prompt': nrm(ks[0], (BATCH, SEQ, D_MODEL)),
        'x_sample': nrm(ks[1], (DEC_BATCH, DEC_SEQ, D_MODEL)),
        'cache_sb_kv': nrm(ks[2], (DEPTH, n_pool, PAGE_SIZE, 2, SB_HEADS, HEAD_DIM)),
        'cache_nsa_kv': nrm(ks[3], (DEPTH, n_pool, PAGE_SIZE, 4, NSA_GROUPS, HEAD_DIM)),
        'cache_win_kv': nrm(ks[4], (DEPTH, DEC_BATCH, win_buf, 2, NSA_GROUPS, HEAD_DIM)),
        'page_table': page_table,
        'c_prompt': nrm(ks[6], (BATCH, D_MODEL)),
        'c_sample': nrm(ks[7], (DEC_BATCH, D_MODEL)),
        'norm1_g': 1.0 + nrm(ks[8], (DEPTH, D_MODEL), 0.02),
        'norm2_g': 1.0 + nrm(ks[9], (DEPTH, D_MODEL), 0.02),
        'w_ada': nrm(ks[10], (DEPTH, D_MODEL, N_ADA * D_MODEL), D_MODEL ** -0.5),
        'b_ada': nrm(ks[11], (DEPTH, N_ADA * D_MODEL), 0.01),
        'w_in': nrm(ks[12], (DEPTH, D_MODEL, IN_WIDTH), D_MODEL ** -0.5),
        'w_up_sb': nrm(ks[13], (DEPTH, SB_WIDTH, D_MODEL), SB_WIDTH ** -0.5),
        'w_up_nsa': nrm(ks[14], (DEPTH, NSA_WIDTH, D_MODEL), NSA_WIDTH ** -0.5),
        'w_o': nrm(ks[15], (DEPTH, D_MODEL, D_MODEL), D_MODEL ** -0.5),
        'peer_wq': nrm(ks[16], (DEPTH, D_MODEL, PEER_HEADS * PEER_DKEY), D_MODEL ** -0.5),
        'peer_keys': nrm(ks[17], (DEPTH, PEER_HEADS, 2, N_KEYS, PEER_HALF), PEER_HALF ** -0.5),
        'peer_u': nrm(ks[18], (DEPTH, N_EXPERTS, D_MODEL), D_MODEL ** -0.5),
        'peer_v': nrm(ks[19], (DEPTH, N_EXPERTS, D_MODEL), PEER_HEADS ** -0.5),
        'final_g': 1.0 + nrm(ks[20], (D_MODEL,), 0.02),
    }


def reference(x_prompt, x_sample, cache_sb_kv, cache_nsa_kv, cache_win_kv, page_table, c_prompt, c_sample,
              norm1_g, norm2_g, w_ada, b_ada, w_in, w_up_sb, w_up_nsa, w_o,
              peer_wq, peer_keys, peer_u, peer_v, final_g):
    hp, hs = x_prompt, x_sample
    sb_p, sb_s, nsa_p, nsa_s, win_p, win_s = [], [], [], [], [], []
    for l in range(DEPTH):
        sh1, sc1, gt1, sh2, sc2, gt2 = adaln(c_prompt, w_ada[l], b_ada[l])
        o_sb, o_nsa, g_sb, g_nsa, kv_sb, kv_nsa, kv_win = prompt_mixer(modulated_norm(hp, norm1_g[l], sh1, sc1), w_in[l])
        hp = hp + gt1 * merge_branches(o_sb, o_nsa, g_sb, g_nsa, w_up_sb[l], w_up_nsa[l], w_o[l])
        hp = hp + gt2 * peer_sublayer(modulated_norm(hp, norm2_g[l], sh2, sc2), peer_wq[l], peer_keys[l], peer_u[l], peer_v[l])
        sb_p.append(kv_sb)
        nsa_p.append(kv_nsa)
        win_p.append(kv_win)
        sh1, sc1, gt1, sh2, sc2, gt2 = adaln(c_sample, w_ada[l], b_ada[l])
        o_sb, o_nsa, g_sb, g_nsa, kv_sb, kv_nsa, kv_win = sample_mixer(
            modulated_norm(hs, norm1_g[l], sh1, sc1), w_in[l], cache_sb_kv, cache_nsa_kv, cache_win_kv, page_table, l)
        hs = hs + gt1 * merge_branches(o_sb, o_nsa, g_sb, g_nsa, w_up_sb[l], w_up_nsa[l], w_o[l])
        hs = hs + gt2 * peer_sublayer(modulated_norm(hs, norm2_g[l], sh2, sc2), peer_wq[l], peer_keys[l], peer_u[l], peer_v[l])
        sb_s.append(kv_sb)
        nsa_s.append(kv_nsa)
        win_s.append(kv_win)
    y_prompt = rmsnorm(hp, final_g)
    y_sample = rmsnorm(hs, final_g)
    sb_kv_prompt, sb_kv_sample = jnp.stack(sb_p), jnp.stack(sb_s)
    nsa_kv_prompt, nsa_kv_sample = jnp.stack(nsa_p), jnp.stack(nsa_s)
    win_kv_prompt, win_kv_sample = jnp.stack(win_p), jnp.stack(win_s)
    return (y_prompt, y_sample, sb_kv_prompt, sb_kv_sample, nsa_kv_prompt, nsa_kv_sample, win_kv_prompt, win_kv_sample)
```

```python
import functools

import jax
import jax.numpy as jnp
import numpy as np
from jax import lax
from jax.experimental import pallas as pl
from jax.experimental.pallas import tpu as pltpu

F32 = jnp.float32
BF16 = jnp.bfloat16

D_MODEL = 2048
HEAD_DIM = 128
SB_HEADS = 8
NSA_HEADS = 8
NSA_GROUPS = 2
NSA_HPG = NSA_HEADS // NSA_GROUPS
CMP_BLOCK = 64
N_SEL = 16
WINDOW = 512
PAGE_SIZE = 128
ROPE_THETA = 10000.0
PEER_HEADS = 8
N_KEYS = 128
PEER_TOPK = 16
RMS_EPS = 1e-6
N_ADA = 6
SCALE = HEAD_DIM ** -0.5
NEG_INF = float("-inf")

VMEM_LIMIT_BYTES = 56 * 1024 * 1024

_PROJ_HEADS = dict(nq=(0, 8), kc=(8, 2), ks=(10, 2), kw=(12, 2), sb_q=(14, 8), sb_k=(22, 8), sb_v=(30, 8),
                   vc=(38, 2), vs=(40, 2), vw=(42, 2), ng=(44, 1))
_N_ROPE_HEADS = 14
_N_PROJ_HEADS = 45
_PROJ_TILE_HEADS = 9


def _cparams(*sem):
    return pltpu.CompilerParams(dimension_semantics=sem, vmem_limit_bytes=VMEM_LIMIT_BYTES)


def _adaln_kernel(c_ref, w_ref, b_ref, o_ref):
    c = c_ref[...]
    a = (c * jax.nn.sigmoid(c)).astype(BF16)
    o_ref[...] = jnp.dot(a, w_ref[...].astype(BF16), preferred_element_type=F32) + b_ref[...]


def adaln_mod(c, w_ada, b_ada, tn=1024):
    m, d = c.shape
    n = w_ada.shape[1]
    return pl.pallas_call(
        _adaln_kernel,
        grid=(n // tn,),
        in_specs=[pl.BlockSpec((m, d), lambda j: (0, 0)),
                  pl.BlockSpec((d, tn), lambda j: (0, j)),
                  pl.BlockSpec((1, tn), lambda j: (0, j))],
        out_specs=pl.BlockSpec((m, tn), lambda j: (0, j)),
        out_shape=jax.ShapeDtypeStruct((m, n), F32),
        compiler_params=_cparams("arbitrary"),
        name="adaln_mod",
    )(c, w_ada, b_ada.reshape(1, n))


def _modnorm_kernel(x_ref, g_ref, sh_ref, sc_ref, o_ref):
    x = x_ref[...]
    y = x * lax.rsqrt(jnp.mean(x * x, axis=-1, keepdims=True) + RMS_EPS)
    y = y * g_ref[...]
    o_ref[...] = (y * (1.0 + sc_ref[...]) + sh_ref[...]).astype(o_ref.dtype)


def modnorm(x, g, mod3, k_shift, k_scale, tm, tiles_per_batch):
    m, d = x.shape
    r = mod3.shape[1]
    return pl.pallas_call(
        _modnorm_kernel,
        grid=(m // tm,),
        in_specs=[pl.BlockSpec((tm, d), lambda i: (i, 0)),
                  pl.BlockSpec((1, d), lambda i: (0, 0)),
                  pl.BlockSpec((None, r, d), lambda i: (i // tiles_per_batch, 0, k_shift)),
                  pl.BlockSpec((None, r, d), lambda i: (i // tiles_per_batch, 0, k_scale))],
        out_specs=pl.BlockSpec((tm, d), lambda i: (i, 0)),
        out_shape=jax.ShapeDtypeStruct((m, d), BF16),
        compiler_params=_cparams("arbitrary"),
        name="modnorm",
    )(x, g.reshape(1, d), mod3, mod3)


def _inproj_kernel(h_ref, w_ref, cos_ref, sin_ref, o_ref):
    j = pl.program_id(1)
    acc = jnp.dot(h_ref[...], w_ref[...], preferred_element_type=F32)
    for hh in range(_PROJ_TILE_HEADS):
        a = acc[:, hh * HEAD_DIM:(hh + 1) * HEAD_DIM]
        head = j * _PROJ_TILE_HEADS + hh

        @pl.when(head < _N_ROPE_HEADS)
        def _():
            o_ref[:, hh * HEAD_DIM:(hh + 1) * HEAD_DIM] = (
                a * cos_ref[...] + pltpu.roll(a, HEAD_DIM // 2, 1) * sin_ref[...])

        @pl.when(head >= _N_ROPE_HEADS)
        def _():
            o_ref[:, hh * HEAD_DIM:(hh + 1) * HEAD_DIM] = a


def in_proj(hn, w_proj, cos, sin, tm):
    m, d = hn.shape
    n = w_proj.shape[1]
    tn = _PROJ_TILE_HEADS * HEAD_DIM
    npos = cos.shape[0] // tm
    return pl.pallas_call(
        _inproj_kernel,
        grid=(m // tm, n // tn),
        in_specs=[pl.BlockSpec((tm, d), lambda i, j: (i, 0)),
                  pl.BlockSpec((d, tn), lambda i, j: (0, j)),
                  pl.BlockSpec((tm, HEAD_DIM), lambda i, j: (i % npos, 0)),
                  pl.BlockSpec((tm, HEAD_DIM), lambda i, j: (i % npos, 0))],
        out_specs=pl.BlockSpec((tm, tn), lambda i, j: (i, j)),
        out_shape=jax.ShapeDtypeStruct((m, n), F32),
        compiler_params=_cparams("arbitrary", "arbitrary"),
        name="in_proj",
    )(hn, w_proj, cos, sin)


def _split_hi_lo(x):
    hi = x.astype(BF16)
    lo = (x - hi.astype(F32)).astype(BF16)
    return hi, lo


def _softplus_terms(z):
    t = jnp.log1p(jnp.exp(-jnp.abs(z)))
    return jnp.minimum(z, 0.0) - t, jnp.minimum(-z, 0.0) - t


def _sb_prompt_kernel(q_ref, k_ref, v_ref, o_ref, *, t):
    qi = pl.program_id(2)
    q = q_ref[...].astype(BF16)
    row = lax.broadcasted_iota(jnp.int32, (t, t), 0)
    col = lax.broadcasted_iota(jnp.int32, (t, t), 1)
    later = (row > col).astype(BF16)

    def body(step, carry):
        acc, c = carry
        k0 = pl.multiple_of((qi - step) * t, t)
        k = k_ref[pl.ds(k0, t), :].astype(BF16)
        v = v_ref[pl.ds(k0, t), :].astype(BF16)
        z = lax.dot_general(q, k, (((1,), (1,)), ((), ())), preferred_element_type=F32) * SCALE
        ls, lr = _softplus_terms(z)
        mask = jnp.logical_or(step > 0, col < row)
        lr = jnp.where(mask, lr, 0.0)
        hi, lo = _split_hi_lo(lr)
        tail = (jnp.dot(hi, later, preferred_element_type=F32)
                + jnp.dot(lo, later, preferred_element_type=F32))
        w = jnp.where(mask, jnp.exp(ls + tail + c), 0.0)
        acc = acc + jnp.dot(w.astype(BF16), v, preferred_element_type=F32)
        c = c + jnp.sum(lr, axis=1, keepdims=True)
        return acc, c

    acc, _ = lax.fori_loop(0, qi + 1, body,
                           (jnp.zeros((t, HEAD_DIM), F32), jnp.zeros((t, 1), F32)))
    o_ref[...] = acc.astype(o_ref.dtype)


def sb_prompt(proj, batch, seq, t=256):
    nq = seq // t
    q0, k0, v0 = _PROJ_HEADS["sb_q"][0], _PROJ_HEADS["sb_k"][0], _PROJ_HEADS["sb_v"][0]
    return pl.pallas_call(
        functools.partial(_sb_prompt_kernel, t=t),
        grid=(batch, SB_HEADS, nq),
        in_specs=[pl.BlockSpec((t, HEAD_DIM), lambda b, h, i: (b * nq + i, q0 + h)),
                  pl.BlockSpec((seq, HEAD_DIM), lambda b, h, i: (b, k0 + h)),
                  pl.BlockSpec((seq, HEAD_DIM), lambda b, h, i: (b, v0 + h))],
        out_specs=pl.BlockSpec((t, HEAD_DIM), lambda b, h, i: (b * nq + i, h)),
        out_shape=jax.ShapeDtypeStruct((batch * seq, SB_HEADS * HEAD_DIM), BF16),
        compiler_params=_cparams("arbitrary", "arbitrary", "arbitrary"),
        name="sb_prompt",
    )(proj, proj, proj)


def _masked_softmax(z, mask):
    z = jnp.where(mask, z, NEG_INF)
    m = jnp.max(z, axis=-1, keepdims=True)
    e = jnp.exp(z - m)
    return e / jnp.sum(e, axis=-1, keepdims=True)


def _top_blocks_mask(imp, cur, n_sel):
    rows, nb = imp.shape
    blk = lax.broadcasted_iota(jnp.int32, (rows, nb), 1)
    impm = jnp.where(blk == cur, jnp.inf, jnp.where(blk > cur, NEG_INF, imp))
    rank = jnp.zeros((rows, nb), jnp.int32)
    for i in range(nb):
        ci = impm[:, i:i + 1]
        ahead = jnp.logical_or(ci > impm, jnp.logical_and(ci == impm, blk > i))
        rank = rank + ahead.astype(jnp.int32)
    return jnp.logical_and(rank < n_sel, blk <= cur)


def _nsa_prompt_kernel(q_ref, kc_ref, vc_ref, ks_ref, vs_ref, kw_ref, vw_ref, ng_ref, o_ref,
                       kb_ref, vb_ref, *, tq, seq):
    g = pl.program_id(1)
    qi = pl.program_id(2)
    nb = seq // CMP_BLOCK
    hq = NSA_HPG * tq

    @pl.when(qi == 0)
    def _():
        kb_ref[...] = jnp.mean(kc_ref[...].reshape(nb, CMP_BLOCK, HEAD_DIM), axis=1)
        vb_ref[...] = jnp.mean(vc_ref[...].reshape(nb, CMP_BLOCK, HEAD_DIM), axis=1)

    s0 = qi * tq
    q4 = q_ref[...]
    qs = jnp.concatenate([q4[:, h * HEAD_DIM:(h + 1) * HEAD_DIM] for h in range(NSA_HPG)], axis=0).astype(BF16)
    qpos = s0 + lax.broadcasted_iota(jnp.int32, (hq, 1), 0) % tq
    qpos_t = s0 + lax.broadcasted_iota(jnp.int32, (tq, 1), 0)
    nt = (((1,), (1,)), ((), ()))

    z = lax.dot_general(qs, kb_ref[...].astype(BF16), nt, preferred_element_type=F32) * SCALE
    blk = lax.broadcasted_iota(jnp.int32, (hq, nb), 1)
    vis = (blk + 1) * CMP_BLOCK - 1 <= qpos
    zmax = jnp.max(jnp.where(vis, z, NEG_INF), axis=-1, keepdims=True)
    zmax = jnp.where(jnp.isfinite(zmax), zmax, 0.0)
    e = jnp.where(vis, jnp.exp(z - zmax), 0.0)
    p = e / jnp.maximum(jnp.sum(e, axis=-1, keepdims=True), jnp.finfo(F32).tiny)
    o_cmp = jnp.dot(p.astype(BF16), vb_ref[...].astype(BF16), preferred_element_type=F32)
    imp = p[0:tq]
    for h in range(1, NSA_HPG):
        imp = imp + p[h * tq:(h + 1) * tq]

    sel = _top_blocks_mask(imp, qpos_t // CMP_BLOCK, min(N_SEL, nb))
    expand = (lax.broadcasted_iota(jnp.int32, (nb, seq), 1) // CMP_BLOCK
              == lax.broadcasted_iota(jnp.int32, (nb, seq), 0)).astype(BF16)
    keymask = jnp.dot(sel.astype(BF16), expand, preferred_element_type=F32) > 0.5
    keymask = jnp.logical_and(keymask, lax.broadcasted_iota(jnp.int32, (tq, seq), 1) <= qpos_t)
    keymask = jnp.concatenate([keymask.astype(F32)] * NSA_HPG, axis=0) > 0.5
    zs = lax.dot_general(qs, ks_ref[...].astype(BF16), nt, preferred_element_type=F32) * SCALE
    ps = _masked_softmax(zs, keymask)
    o_sel = jnp.dot(ps.astype(BF16), vs_ref[...].astype(BF16), preferred_element_type=F32)

    span = WINDOW + tq
    w0 = pl.multiple_of(jnp.maximum(s0 - WINDOW, 0), tq)
    kwin = kw_ref[pl.ds(w0, span), :].astype(BF16)
    vwin = vw_ref[pl.ds(w0, span), :].astype(BF16)
    rel = qpos - (w0 + lax.broadcasted_iota(jnp.int32, (hq, span), 1))
    zw = lax.dot_general(qs, kwin, nt, preferred_element_type=F32) * SCALE
    pw = _masked_softmax(zw, jnp.logical_and(rel >= 0, rel < WINDOW))
    o_win = jnp.dot(pw.astype(BF16), vwin, preferred_element_type=F32)

    gate = jax.nn.sigmoid(ng_ref[...])
    lane = lax.broadcasted_iota(jnp.int32, (tq, HEAD_DIM), 1)
    outs = []
    for h in range(NSA_HPG):
        base = (g * NSA_HPG + h) * 3
        gs = [jnp.sum(jnp.where(lane == base + br, gate, 0.0), axis=-1, keepdims=True) for br in range(3)]
        rows = slice(h * tq, (h + 1) * tq)
        outs.append(gs[0] * o_cmp[rows] + gs[1] * o_sel[rows] + gs[2] * o_win[rows])
    o_ref[...] = jnp.concatenate(outs, axis=1).astype(o_ref.dtype)


def nsa_prompt(proj, batch, seq, tq=128):
    nq = seq // tq
    nb = seq // CMP_BLOCK
    hd = _PROJ_HEADS

    def kv_spec(name):
        first = hd[name][0]
        return pl.BlockSpec((seq, HEAD_DIM), lambda b, g, i: (b, first + g))

    return pl.pallas_call(
        functools.partial(_nsa_prompt_kernel, tq=tq, seq=seq),
        grid=(batch, NSA_GROUPS, nq),
        in_specs=[pl.BlockSpec((tq, NSA_HPG * HEAD_DIM), lambda b, g, i: (b * nq + i, g)),
                  kv_spec("kc"), kv_spec("vc"), kv_spec("ks"), kv_spec("vs"), kv_spec("kw"), kv_spec("vw"),
                  pl.BlockSpec((tq, HEAD_DIM), lambda b, g, i: (b * nq + i, hd["ng"][0]))],
        out_specs=pl.BlockSpec((tq, NSA_HPG * HEAD_DIM), lambda b, g, i: (b * nq + i, g)),
        out_shape=jax.ShapeDtypeStruct((batch * seq, NSA_HEADS * HEAD_DIM), BF16),
        scratch_shapes=[pltpu.VMEM((nb, HEAD_DIM), F32), pltpu.VMEM((nb, HEAD_DIM), F32)],
        compiler_params=_cparams("arbitrary", "arbitrary", "arbitrary"),
        name="nsa_prompt",
    )(proj, proj, proj, proj, proj, proj, proj, proj)


def _merge_kernel(h_ref, osb_ref, onsa_ref, wgs_ref, wgn_ref, wus_ref, wun_ref, o_ref):
    h = h_ref[...]
    gs = jax.nn.sigmoid(jnp.dot(h, wgs_ref[...], preferred_element_type=F32))
    gn = jax.nn.sigmoid(jnp.dot(h, wgn_ref[...], preferred_element_type=F32))
    us = jnp.dot(osb_ref[...], wus_ref[...], preferred_element_type=F32)
    un = jnp.dot(onsa_ref[...], wun_ref[...], preferred_element_type=F32)
    o_ref[...] = (gs * us + gn * un).astype(o_ref.dtype)


def merge_branches(hn, o_sb, o_nsa, wg_sb, wg_nsa, wu_sb, wu_nsa, tm, tn=512):
    m, d = hn.shape
    w = o_sb.shape[1]
    return pl.pallas_call(
        _merge_kernel,
        grid=(m // tm, d // tn),
        in_specs=[pl.BlockSpec((tm, d), lambda i, j: (i, 0)),
                  pl.BlockSpec((tm, w), lambda i, j: (i, 0)),
                  pl.BlockSpec((tm, w), lambda i, j: (i, 0)),
                  pl.BlockSpec((d, tn), lambda i, j: (0, j)),
                  pl.BlockSpec((d, tn), lambda i, j: (0, j)),
                  pl.BlockSpec((w, tn), lambda i, j: (0, j)),
                  pl.BlockSpec((w, tn), lambda i, j: (0, j))],
        out_specs=pl.BlockSpec((tm, tn), lambda i, j: (i, j)),
        out_shape=jax.ShapeDtypeStruct((m, d), BF16),
        compiler_params=_cparams("arbitrary", "arbitrary"),
        name="merge_branches",
    )(hn, o_sb, o_nsa, wg_sb, wg_nsa, wu_sb, wu_nsa)


def _oproj_kernel(a_ref, w_ref, x_ref, gt_ref, o_ref):
    o_ref[...] = x_ref[...] + gt_ref[...] * jnp.dot(a_ref[...], w_ref[...], preferred_element_type=F32)


def oproj_residual(a, w, x, mod3, k_gate, tm, tiles_per_batch, tn=1024):
    m, d = x.shape
    r = mod3.shape[1]
    nj = d // tn
    return pl.pallas_call(
        _oproj_kernel,
        grid=(m // tm, nj),
        in_specs=[pl.BlockSpec((tm, a.shape[1]), lambda i, j: (i, 0)),
                  pl.BlockSpec((a.shape[1], tn), lambda i, j: (0, j)),
                  pl.BlockSpec((tm, tn), lambda i, j: (i, j)),
                  pl.BlockSpec((None, r, tn), lambda i, j: (i // tiles_per_batch, 0, k_gate * nj + j))],
        out_specs=pl.BlockSpec((tm, tn), lambda i, j: (i, j)),
        out_shape=jax.ShapeDtypeStruct((m, d), F32),
        compiler_params=_cparams("arbitrary", "arbitrary"),
        name="oproj_residual",
    )(a, w, x, mod3)


def _matmul_kernel(a_ref, w_ref, o_ref):
    o_ref[...] = jnp.dot(a_ref[...], w_ref[...], preferred_element_type=F32)


def matmul(a, w, tm, tn=1024):
    m, k = a.shape
    n = w.shape[1]
    return pl.pallas_call(
        _matmul_kernel,
        grid=(m // tm, n // tn),
        in_specs=[pl.BlockSpec((tm, k), lambda i, j: (i, 0)),
                  pl.BlockSpec((k, tn), lambda i, j: (0, j))],
        out_specs=pl.BlockSpec((tm, tn), lambda i, j: (i, j)),
        out_shape=jax.ShapeDtypeStruct((m, n), F32),
        compiler_params=_cparams("arbitrary", "arbitrary"),
        name="matmul",
    )(a, w)


def _extract_top(x, k):
    n = x.shape[0]
    ridx = lax.broadcasted_iota(jnp.int32, x.shape, 0)
    picked = jnp.zeros(x.shape, jnp.bool_)
    vals = []
    for _ in range(k):
        m = jnp.max(x, axis=0, keepdims=True)
        first = jnp.min(jnp.where(x == m, ridx, n), axis=0, keepdims=True)
        hit = ridx == first
        vals.append(m)
        picked = jnp.logical_or(picked, hit)
        x = jnp.where(hit, NEG_INF, x)
    return jnp.concatenate(vals, axis=0), picked


def _peer_route_kernel(q_ref, keys_ref, s0_ref, s1_ref, a0_ref, b1_ref, tau_ref):
    nt = (((1,), (1,)), ((), ()))
    for h in range(PEER_HEADS):
        s, vals, picked = [], [], []
        for c in range(2):
            col = (h * 2 + c) * HEAD_DIM
            qhc = q_ref[:, col:col + HEAD_DIM].astype(BF16)
            sc = lax.dot_general(keys_ref[h * 2 + c].astype(BF16), qhc, nt, preferred_element_type=F32)
            v, pk = _extract_top(sc, PEER_TOPK)
            s.append(sc)
            vals.append(v)
            picked.append(pk)
        cand = jnp.concatenate([vals[0][a:a + 1] + vals[1] for a in range(PEER_TOPK)], axis=0)
        fin, _ = _extract_top(cand, PEER_TOPK)
        top = fin[0:1]
        z = jnp.sum(jnp.exp(fin - top), axis=0, keepdims=True)
        s0_ref[h] = jnp.where(picked[0], s[0], NEG_INF)
        s1_ref[h] = jnp.where(picked[1], s[1], NEG_INF)
        a0_ref[h] = jnp.where(picked[0], jnp.exp(s[0] - vals[0][0:1]), 0.0) / z
        b1_ref[h] = jnp.where(picked[1], jnp.exp(s[1] - vals[1][0:1]), 0.0)
        tau_ref[h] = jnp.broadcast_to(fin[PEER_TOPK - 1:PEER_TOPK], tau_ref.shape[1:])


def peer_route(qp, keys, tm):
    m = qp.shape[0]
    big = jax.ShapeDtypeStruct((PEER_HEADS, N_KEYS, m), F32)
    bspec = pl.BlockSpec((PEER_HEADS, N_KEYS, tm), lambda i: (0, 0, i))
    return pl.pallas_call(
        _peer_route_kernel,
        grid=(m // tm,),
        in_specs=[pl.BlockSpec((tm, qp.shape[1]), lambda i: (i, 0)),
                  pl.BlockSpec((2 * PEER_HEADS, N_KEYS, HEAD_DIM), lambda i: (0, 0, 0))],
        out_specs=[bspec, bspec, bspec, bspec, pl.BlockSpec((PEER_HEADS, 8, tm), lambda i: (0, 0, i))],
        out_shape=[big, big, big, big, jax.ShapeDtypeStruct((PEER_HEADS, 8, m), F32)],
        compiler_params=_cparams("arbitrary"),
        name="peer_route",
    )(qp, keys.reshape(2 * PEER_HEADS, N_KEYS, HEAD_DIM))


def _peer_dense_kernel(h_ref, u_ref, v_ref, s0_ref, s1_ref, a0_ref, b1_ref, tau_ref, o_ref, *, te):
    e = pl.program_id(1)

    @pl.when(e == 0)
    def _():
        o_ref[...] = jnp.zeros_like(o_ref)

    nt = (((1,), (1,)), ((), ()))
    act_in = lax.dot_general(u_ref[...], h_ref[...], nt, preferred_element_type=F32)
    act = 0.5 * act_in * (1.0 + lax.erf(act_in * (2.0 ** -0.5)))
    blocks = []
    for ii in range(te // N_KEYS):
        i = e * (te // N_KEYS) + ii
        wt = None
        for h in range(PEER_HEADS):
            s0 = s0_ref[h, pl.ds(i, 1), :]
            a0 = a0_ref[h, pl.ds(i, 1), :]
            keep = (s0 + s1_ref[h]) >= tau_ref[h, 0:1, :]
            term = jnp.where(keep, a0 * b1_ref[h], 0.0)
            wt = term if wt is None else wt + term
        blocks.append(wt)
    wt = jnp.concatenate(blocks, axis=0)
    gated = (wt * act).astype(BF16)
    o_ref[...] += lax.dot_general(gated, v_ref[...], (((0,), (0,)), ((), ())), preferred_element_type=F32)


def peer_dense(hn, u, v, route, tm, te=512):
    m, d = hn.shape
    n_exp = u.shape[0]
    s0, s1, a0, b1, tau = route
    rspec = pl.BlockSpec((PEER_HEADS, N_KEYS, tm), lambda i, e: (0, 0, i))
    return pl.pallas_call(
        functools.partial(_peer_dense_kernel, te=te),
        grid=(m // tm, n_exp // te),
        in_specs=[pl.BlockSpec((tm, d), lambda i, e: (i, 0)),
                  pl.BlockSpec((te, d), lambda i, e: (e, 0)),
                  pl.BlockSpec((te, d), lambda i, e: (e, 0)),
                  rspec, rspec, rspec, rspec,
                  pl.BlockSpec((PEER_HEADS, 8, tm), lambda i, e: (0, 0, i))],
        out_specs=pl.BlockSpec((tm, d), lambda i, e: (i, 0)),
        out_shape=jax.ShapeDtypeStruct((m, d), F32),
        compiler_params=_cparams("arbitrary", "arbitrary"),
        name="peer_dense",
    )(hn, u, v, s0, s1, a0, b1, tau)


def _final_kernel(x_ref, y_ref, gt_ref, g_ref, o_ref):
    h = x_ref[...] + gt_ref[...] * y_ref[...]
    o_ref[...] = h * lax.rsqrt(jnp.mean(h * h, axis=-1, keepdims=True) + RMS_EPS) * g_ref[...]


def final_norm(x, y, mod3, k_gate, g, tm, tiles_per_batch):
    m, d = x.shape
    r = mod3.shape[1]
    return pl.pallas_call(
        _final_kernel,
        grid=(m // tm,),
        in_specs=[pl.BlockSpec((tm, d), lambda i: (i, 0)),
                  pl.BlockSpec((tm, d), lambda i: (i, 0)),
                  pl.BlockSpec((None, r, d), lambda i: (i // tiles_per_batch, 0, k_gate)),
                  pl.BlockSpec((1, d), lambda i: (0, 0))],
        out_specs=pl.BlockSpec((tm, d), lambda i: (i, 0)),
        out_shape=jax.ShapeDtypeStruct((m, d), F32),
        compiler_params=_cparams("arbitrary"),
        name="final_norm",
    )(x, y, mod3, g.reshape(1, d))


def _token_lane_onehot(n):
    tok = lax.broadcasted_iota(jnp.int32, (n, 1, n), 0)
    lane = lax.broadcasted_iota(jnp.int32, (n, 1, n), 2)
    return tok == lane


def _sb_decode_kernel(pt_ref, q_ref, k_ref, v_ref, o_ref, acc_ref, c_ref):
    del pt_ref
    i = pl.program_id(1)

    @pl.when(i == 0)
    def _():
        acc_ref[...] = jnp.zeros_like(acc_ref)
        c_ref[...] = jnp.zeros_like(c_ref)

    n = k_ref.shape[0]
    onehot = _token_lane_onehot(n)
    zs = jnp.sum(k_ref[...] * q_ref[...][None], axis=-1, keepdims=True) * SCALE
    z = jnp.sum(jnp.where(onehot, zs, 0.0), axis=0)
    ls, lr = _softplus_terms(z)
    row = lax.broadcasted_iota(jnp.int32, (n, n), 0)
    col = lax.broadcasted_iota(jnp.int32, (n, n), 1)
    later = (row > col).astype(BF16)
    hi, lo = _split_hi_lo(lr)
    tail = jnp.dot(hi, later, preferred_element_type=F32) + jnp.dot(lo, later, preferred_element_type=F32)
    w = jnp.exp(ls + tail + c_ref[...])
    wc = jnp.sum(jnp.where(onehot, w[None], 0.0), axis=-1, keepdims=True)
    acc_ref[...] += jnp.sum(wc * v_ref[...], axis=0)
    c_ref[...] += jnp.sum(lr, axis=1, keepdims=True)

    @pl.when(i == pl.num_programs(1) - 1)
    def _():
        o_ref[...] = acc_ref[...]


def sb_decode(q, cache5, page_table):
    bsz, nh, hd = q.shape
    n_pages = page_table.shape[1]
    page = cache5.shape[1]

    def kv_spec(which):
        return pl.BlockSpec((None, page, None, nh, hd), lambda b, i, pt: (pt[b, n_pages - 1 - i], 0, which, 0, 0))

    return pl.pallas_call(
        _sb_decode_kernel,
        grid_spec=pltpu.PrefetchScalarGridSpec(
            num_scalar_prefetch=1,
            grid=(bsz, n_pages),
            in_specs=[pl.BlockSpec((None, nh, hd), lambda b, i, pt: (b, 0, 0)), kv_spec(0), kv_spec(1)],
            out_specs=pl.BlockSpec((None, nh, hd), lambda b, i, pt: (b, 0, 0)),
            scratch_shapes=[pltpu.VMEM((nh, hd), F32), pltpu.VMEM((nh, hd), F32)]),
        out_shape=jax.ShapeDtypeStruct((bsz, nh, hd), F32),
        compiler_params=_cparams("arbitrary", "arbitrary"),
        name="sb_decode",
    )(page_table, q, cache5, cache5)


def _nsa_compress_kernel(pt_ref, x_ref, o_ref):
    del pt_ref
    p = pl.program_id(1)
    x = x_ref[...]
    per_page = x.shape[0] // CMP_BLOCK
    for part in range(per_page):
        s = jnp.sum(x[part * CMP_BLOCK:(part + 1) * CMP_BLOCK], axis=0) * (1.0 / CMP_BLOCK)
        blk = p * per_page + part
        for r in range(2 * NSA_GROUPS):
            o_ref[r, pl.ds(blk, 1), :] = s[r:r + 1, :]


def nsa_compress(cache4, page_table):
    bsz, n_pages = page_table.shape
    page, rows, hd = cache4.shape[1:]
    nb = n_pages * page // CMP_BLOCK
    return pl.pallas_call(
        _nsa_compress_kernel,
        grid_spec=pltpu.PrefetchScalarGridSpec(
            num_scalar_prefetch=1,
            grid=(bsz, n_pages),
            in_specs=[pl.BlockSpec((None, page, rows, hd), lambda b, p, pt: (pt[b, p], 0, 0, 0))],
            out_specs=pl.BlockSpec((None, 2 * NSA_GROUPS, nb, hd), lambda b, p, pt: (b, 0, 0, 0))),
        out_shape=jax.ShapeDtypeStruct((bsz, 2 * NSA_GROUPS, nb, hd), F32),
        compiler_params=_cparams("arbitrary", "arbitrary"),
        name="nsa_compress",
    )(page_table, cache4)


def _by_group(rowg, fn):
    return jnp.where(rowg == 0, fn(0), fn(1))


def _nsa_decode_cmp_kernel(q_ref, kvb_ref, win_ref, nw_ref, ocmp_ref, owin_ref, idx_ref, *, qpos, win_len):
    nt = (((1,), (1,)), ((), ()))
    q = q_ref[...]
    qb = q.astype(BF16)
    rowg = lax.broadcasted_iota(jnp.int32, (NSA_HEADS, 1), 0) // NSA_HPG
    nb = kvb_ref.shape[1]

    z = _by_group(rowg, lambda g: lax.dot_general(qb, kvb_ref[g].astype(BF16), nt, preferred_element_type=F32)) * SCALE
    blk = lax.broadcasted_iota(jnp.int32, (NSA_HEADS, nb), 1)
    vis = (blk + 1) * CMP_BLOCK - 1 <= qpos
    zmax = jnp.max(jnp.where(vis, z, NEG_INF), axis=-1, keepdims=True)
    zmax = jnp.where(jnp.isfinite(zmax), zmax, 0.0)
    e = jnp.where(vis, jnp.exp(z - zmax), 0.0)
    p = e / jnp.maximum(jnp.sum(e, axis=-1, keepdims=True), jnp.finfo(F32).tiny)
    pb = p.astype(BF16)
    ocmp_ref[...] = _by_group(rowg, lambda g: jnp.dot(pb, kvb_ref[NSA_GROUPS + g].astype(BF16),
                                                     preferred_element_type=F32))

    eye = (lax.broadcasted_iota(jnp.int32, (nb, nb), 0) == lax.broadcasted_iota(jnp.int32, (nb, nb), 1))
    before = lax.broadcasted_iota(jnp.int32, (nb, nb), 0) < lax.broadcasted_iota(jnp.int32, (nb, nb), 1)
    slot = lax.broadcasted_iota(jnp.int32, (N_SEL, nb), 0)
    lane = lax.broadcasted_iota(jnp.int32, (N_SEL, nb), 1).astype(F32)
    for g in range(NSA_GROUPS):
        imp = jnp.sum(p[g * NSA_HPG:(g + 1) * NSA_HPG], axis=0, keepdims=True)
        impcol = jnp.sum(jnp.where(eye, imp, 0.0), axis=1, keepdims=True)
        ahead = jnp.logical_or(impcol > imp, jnp.logical_and(impcol == imp, before))
        rank = jnp.sum(ahead.astype(F32), axis=0, keepdims=True)
        chosen = jnp.sum(jnp.where(rank == slot.astype(F32), lane, 0.0), axis=1, keepdims=True)
        idx_ref[g] = jnp.broadcast_to(chosen, (N_SEL, nb)).astype(jnp.int32)

    nw = nw_ref[...]
    zw = _by_group(rowg, lambda g: lax.dot_general(
        qb, win_ref[pl.ds(g, win_len, stride=2 * NSA_GROUPS), :].astype(BF16), nt, preferred_element_type=F32)) * SCALE
    kpos = qpos - win_len + lax.broadcasted_iota(jnp.int32, (NSA_HEADS, win_len), 1)
    rel = qpos - kpos
    wmask = jnp.logical_and(jnp.logical_and(rel >= 0, rel < WINDOW), kpos >= 0)
    k_new = _by_group(rowg, lambda g: nw[g:g + 1])
    v_new = _by_group(rowg, lambda g: nw[NSA_GROUPS + g:NSA_GROUPS + g + 1])
    z_new = jnp.sum(q * k_new, axis=-1, keepdims=True) * SCALE
    m = jnp.maximum(jnp.max(jnp.where(wmask, zw, NEG_INF), axis=-1, keepdims=True), z_new)
    ew = jnp.where(wmask, jnp.exp(zw - m), 0.0)
    en = jnp.exp(z_new - m)
    ewb = ew.astype(BF16)
    num = _by_group(rowg, lambda g: jnp.dot(
        ewb, win_ref[pl.ds(NSA_GROUPS + g, win_len, stride=2 * NSA_GROUPS), :].astype(BF16),
        preferred_element_type=F32))
    owin_ref[...] = (num + en * v_new) / (jnp.sum(ew, axis=-1, keepdims=True) + en)


def nsa_decode_cmp(q, kvb, win2d, new_win, qpos):
    bsz, nh, hd = q.shape
    nb = kvb.shape[2]
    win_len = win2d.shape[1] // (2 * NSA_GROUPS)
    vec = jax.ShapeDtypeStruct((bsz, nh, hd), F32)
    vspec = pl.BlockSpec((None, nh, hd), lambda b: (b, 0, 0))
    return pl.pallas_call(
        functools.partial(_nsa_decode_cmp_kernel, qpos=qpos, win_len=win_len),
        grid=(bsz,),
        in_specs=[vspec,
                  pl.BlockSpec((None,) + kvb.shape[1:], lambda b: (b, 0, 0, 0)),
                  pl.BlockSpec((None,) + win2d.shape[1:], lambda b: (b, 0, 0)),
                  pl.BlockSpec((None,) + new_win.shape[1:], lambda b: (b, 0, 0))],
        out_specs=[vspec, vspec, pl.BlockSpec((None, NSA_GROUPS, N_SEL, nb), lambda b: (b, 0, 0, 0))],
        out_shape=[vec, vec, jax.ShapeDtypeStruct((bsz, NSA_GROUPS, N_SEL, nb), jnp.int32)],
        compiler_params=_cparams("arbitrary"),
        name="nsa_decode_cmp",
    )(q, kvb, win2d, new_win)


def _nsa_decode_sel_kernel(pt_ref, si_ref, q_ref, x0_ref, x1_ref, nn_ref, ocmp_ref, owin_ref, ng_ref, o_ref,
                           m_ref, l_ref, acc_ref):
    del pt_ref, si_ref
    nt = (((1,), (1,)), ((), ()))
    k = pl.program_id(1)
    q = q_ref[...]
    qb = q.astype(BF16)
    rowg = lax.broadcasted_iota(jnp.int32, (NSA_HEADS, 1), 0) // NSA_HPG
    rows = 4 * NSA_GROUPS
    xs = (x0_ref, x1_ref)

    @pl.when(k == 0)
    def _():
        nn = nn_ref[...]
        k_new = _by_group(rowg, lambda g: nn[2 * NSA_GROUPS + g:2 * NSA_GROUPS + g + 1])
        v_new = _by_group(rowg, lambda g: nn[3 * NSA_GROUPS + g:3 * NSA_GROUPS + g + 1])
        m_ref[...] = jnp.broadcast_to(jnp.sum(q * k_new, axis=-1, keepdims=True) * SCALE, m_ref.shape)
        l_ref[...] = jnp.ones_like(l_ref)
        acc_ref[...] = v_new

    z = _by_group(rowg, lambda g: lax.dot_general(
        qb, xs[g][pl.ds(2 * NSA_GROUPS + g, CMP_BLOCK, stride=rows), :].astype(BF16), nt,
        preferred_element_type=F32)) * SCALE
    m_old = m_ref[...]
    m_new = jnp.maximum(m_old, jnp.max(z, axis=-1, keepdims=True))
    alpha = jnp.exp(m_old - m_new)
    p = jnp.exp(z - m_new[:, 0:1])
    pb = p.astype(BF16)
    pv = _by_group(rowg, lambda g: jnp.dot(
        pb, xs[g][pl.ds(3 * NSA_GROUPS + g, CMP_BLOCK, stride=rows), :].astype(BF16), preferred_element_type=F32))
    l_ref[...] = alpha * l_ref[...] + jnp.sum(p, axis=-1, keepdims=True)
    acc_ref[...] = alpha * acc_ref[...] + pv
    m_ref[...] = m_new

    @pl.when(k == pl.num_programs(1) - 1)
    def _():
        gate = jax.nn.sigmoid(ng_ref[...])
        o_sel = acc_ref[...] / l_ref[...]
        o_ref[...] = gate[0] * ocmp_ref[...] + gate[1] * o_sel + gate[2] * owin_ref[...]


def nsa_decode_sel(q, cache_half, page_table, sel_idx, new_nsa, o_cmp, o_win, ng3):
    bsz, nh, hd = q.shape
    half_rows = cache_half.shape[2]
    blocks_per_page = cache_half.shape[1]
    vspec = pl.BlockSpec((None, nh, hd), lambda b, k, pt, si: (b, 0, 0))

    def blk_spec(g):
        def index(b, k, pt, si):
            blk = si[b, g * N_SEL + k]
            return (pt[b, blk // blocks_per_page], blk % blocks_per_page, 0, 0)
        return pl.BlockSpec((None, None, half_rows, hd), index)

    return pl.pallas_call(
        _nsa_decode_sel_kernel,
        grid_spec=pltpu.PrefetchScalarGridSpec(
            num_scalar_prefetch=2,
            grid=(bsz, N_SEL - 1),
            in_specs=[vspec, blk_spec(0), blk_spec(1), vspec, vspec, vspec,
                      pl.BlockSpec((None, 3, nh, hd), lambda b, k, pt, si: (b, 0, 0, 0))],
            out_specs=vspec,
            scratch_shapes=[pltpu.VMEM((nh, hd), F32)] * 3),
        out_shape=jax.ShapeDtypeStruct((bsz, nh, hd), F32),
        compiler_params=_cparams("arbitrary", "arbitrary"),
        name="nsa_decode_sel",
    )(page_table, sel_idx, q, cache_half, cache_half, new_nsa, o_cmp, o_win, ng3)


def _rope_tables(pos):
    half = HEAD_DIM // 2
    inv_freq = ROPE_THETA ** (-jnp.arange(half, dtype=F32) / half)
    ang = pos.astype(F32)[:, None] * inv_freq[None, :]
    cos, sin = jnp.cos(ang), jnp.sin(ang)
    return jnp.concatenate([cos, cos], axis=-1), jnp.concatenate([-sin, sin], axis=-1)


def _head_cols(proj, name):
    first, n = _PROJ_HEADS[name]
    return proj[:, first * HEAD_DIM:(first + n) * HEAD_DIM]


def kernel(x_prompt, x_sample, cache_sb_kv, cache_nsa_kv, cache_win_kv, page_table, c_prompt, c_sample, norm1_g, norm2_g, w_ada, b_ada, w_in, w_up_sb, w_up_nsa, w_o, peer_wq, peer_keys, peer_u, peer_v, final_g):
    batch, seq, d = x_prompt.shape
    dec_batch = x_sample.shape[0]
    n_pool, page = cache_sb_kv.shape[1:3]
    past = page_table.shape[1] * page
    layer = 0

    w = w_in[layer]
    off = dict(sb_q=0, sb_k=1024, sb_v=2048, nq=3072, kc=4096, vc=4352, ks=4608, vs=4864, kw=5120, vw=5376)
    order = ("nq", "kc", "ks", "kw", "sb_q", "sb_k", "sb_v", "vc", "vs", "vw")
    n_gate = 3 * NSA_HEADS
    pieces = [w[:, off[n]:off[n] + _PROJ_HEADS[n][1] * HEAD_DIM] for n in order]
    pieces.append(jnp.pad(w[:, 5632:5632 + n_gate], ((0, 0), (0, HEAD_DIM - n_gate))))
    w_proj = jnp.concatenate(pieces, axis=1).astype(BF16)
    wg_sb = w[:, 5632 + n_gate:5632 + n_gate + d].astype(BF16)
    wg_nsa = w[:, 5632 + n_gate + d:].astype(BF16)
    wu_sb, wu_nsa = w_up_sb[layer].astype(BF16), w_up_nsa[layer].astype(BF16)
    wo, wq = w_o[layer].astype(BF16), peer_wq[layer].astype(BF16)
    u, v = peer_u[layer].astype(BF16), peer_v[layer].astype(BF16)
    keys = peer_keys[layer]

    n_mod = batch + dec_batch
    c_all = jnp.pad(jnp.concatenate([c_prompt, c_sample], axis=0), ((0, (-n_mod) % 8), (0, 0)))
    mod = adaln_mod(c_all, w_ada[layer], b_ada[layer])
    mod_p = mod[:batch].reshape(batch, 1, N_ADA * d)
    mod_s = mod[batch:n_mod].reshape(1, dec_batch, N_ADA * d)

    tm = 1024
    tpb = seq // tm
    xp = x_prompt.reshape(batch * seq, d)
    cos_p, sin_p = _rope_tables(jnp.arange(seq, dtype=jnp.int32))
    hn = modnorm(xp, norm1_g[layer], mod_p, 0, 1, 512, seq // 512)
    proj = in_proj(hn, w_proj, cos_p, sin_p, tm)
    o_sb = sb_prompt(proj, batch, seq)
    o_nsa = nsa_prompt(proj, batch, seq)
    merged = merge_branches(hn, o_sb, o_nsa, wg_sb, wg_nsa, wu_sb, wu_nsa, tm)
    h1 = oproj_residual(merged, wo, xp, mod_p, 2, tm, tpb)
    hn2 = modnorm(h1, norm2_g[layer], mod_p, 3, 4, 512, seq // 512)
    qp = matmul(hn2, wq, tm)
    y_p = peer_dense(hn2, u, v, peer_route(qp, keys, 256), 512)
    y_prompt = final_norm(h1, y_p, mod_p, 5, final_g, 512, seq // 512)

    proj3 = proj.reshape(batch, seq, _N_PROJ_HEADS * HEAD_DIM)
    sb_kv_prompt = jnp.concatenate([_head_cols(proj, "sb_k"), _head_cols(proj, "sb_v")], axis=1)
    nsa_kv_prompt = jnp.concatenate([_head_cols(proj, n) for n in ("kc", "vc", "ks", "vs")], axis=1)
    win_len_p = min(WINDOW, seq)
    tail = proj3[:, seq - win_len_p:].reshape(batch * win_len_p, -1)
    win_kv_prompt = jnp.concatenate([_head_cols(tail, "kw"), _head_cols(tail, "vw")], axis=1)

    xs = x_sample.reshape(dec_batch, d)
    cos_s, sin_s = _rope_tables(jnp.full((dec_batch,), past, jnp.int32))
    hn_s = modnorm(xs, norm1_g[layer], mod_s, 0, 1, dec_batch, 1)
    proj_s = in_proj(hn_s, w_proj, cos_s, sin_s, dec_batch)
    new_sb = jnp.concatenate([_head_cols(proj_s, "sb_k"), _head_cols(proj_s, "sb_v")], axis=1)
    new_nsa = jnp.concatenate([_head_cols(proj_s, n) for n in ("kc", "vc", "ks", "vs")], axis=1)
    new_win = jnp.concatenate([_head_cols(proj_s, "kw"), _head_cols(proj_s, "vw")], axis=1)

    cache_sb5 = cache_sb_kv[layer].reshape(n_pool, page, 2, SB_HEADS, HEAD_DIM)
    q_sb = _head_cols(proj_s, "sb_q").reshape(dec_batch, SB_HEADS, HEAD_DIM)
    o_sb_s = sb_decode(q_sb, cache_sb5, page_table).reshape(dec_batch, SB_HEADS * HEAD_DIM).astype(BF16)

    rows = 4 * NSA_GROUPS
    cache_nsa4 = cache_nsa_kv[layer].reshape(n_pool, page, rows, HEAD_DIM)
    cache_half = cache_nsa_kv[layer].reshape(n_pool, page // CMP_BLOCK, CMP_BLOCK * rows, HEAD_DIM)
    win_buf = cache_win_kv.shape[2]
    win2d = cache_win_kv[layer].reshape(dec_batch, win_buf * 2 * NSA_GROUPS, HEAD_DIM)
    q_nsa = _head_cols(proj_s, "nq").reshape(dec_batch, NSA_HEADS, HEAD_DIM)
    kvb = nsa_compress(cache_nsa4, page_table)
    o_cmp, o_win, sel = nsa_decode_cmp(q_nsa, kvb, win2d, new_win.reshape(dec_batch, 2 * NSA_GROUPS, HEAD_DIM), past)
    sel_idx = sel[:, :, :, 0].reshape(dec_batch, NSA_GROUPS * N_SEL)
    ng3 = _head_cols(proj_s, "ng")[:, :n_gate].reshape(dec_batch, NSA_HEADS, 3).transpose(0, 2, 1)
    ng3 = jnp.broadcast_to(ng3[..., None], (dec_batch, 3, NSA_HEADS, HEAD_DIM))
    o_nsa_s = nsa_decode_sel(q_nsa, cache_half, page_table, sel_idx, new_nsa.reshape(dec_batch, rows, HEAD_DIM),
                             o_cmp, o_win, ng3).reshape(dec_batch, NSA_HEADS * HEAD_DIM).astype(BF16)

    merged_s = merge_branches(hn_s, o_sb_s, o_nsa_s, wg_sb, wg_nsa, wu_sb, wu_nsa, dec_batch)
    h1_s = oproj_residual(merged_s, wo, xs, mod_s, 2, dec_batch, 1)
    hn2_s = modnorm(h1_s, norm2_g[layer], mod_s, 3, 4, dec_batch, 1)
    lane_pad = (-dec_batch) % HEAD_DIM
    hn2_sp = jnp.pad(hn2_s, ((0, lane_pad), (0, 0)))
    tm_s = dec_batch + lane_pad
    qp_s = matmul(hn2_sp, wq, tm_s)
    y_s = peer_dense(hn2_sp, u, v, peer_route(qp_s, keys, tm_s), tm_s)
    y_sample = final_norm(h1_s, y_s[:dec_batch], mod_s, 5, final_g, dec_batch, 1)

    win_kv_sample = jnp.concatenate(
        [cache_win_kv[layer][:, 1:], new_win.reshape(dec_batch, 1, 2, NSA_GROUPS, HEAD_DIM)], axis=1)
    return (y_prompt.reshape(batch, seq, d),
            y_sample.reshape(dec_batch, 1, d),
            sb_kv_prompt.reshape(1, batch, seq, 2, SB_HEADS, HEAD_DIM),
            new_sb.reshape(1, dec_batch, 1, 2, SB_HEADS, HEAD_DIM),
            nsa_kv_prompt.reshape(1, batch, seq, 4, NSA_GROUPS, HEAD_DIM),
            new_nsa.reshape(1, dec_batch, 1, 4, NSA_GROUPS, HEAD_DIM),
            win_kv_prompt.reshape(1, batch, win_len_p, 2, NSA_GROUPS, HEAD_DIM),
            win_kv_sample[None])
```

```python
import functools

import jax
import jax.numpy as jnp
import numpy as np
from jax import lax
from jax.experimental import pallas as pl
from jax.experimental.pallas import tpu as pltpu

F32 = jnp.float32
BF16 = jnp.bfloat16

D_MODEL = 2048
HEAD_DIM = 128
SB_HEADS = 8
NSA_HEADS = 8
NSA_GROUPS = 2
NSA_HPG = NSA_HEADS // NSA_GROUPS
CMP_BLOCK = 64
N_SEL = 16
WINDOW = 512
PAGE_SIZE = 128
ROPE_THETA = 10000.0
PEER_HEADS = 8
N_KEYS = 128
PEER_TOPK = 16
RMS_EPS = 1e-6
N_ADA = 6
SCALE = HEAD_DIM ** -0.5
NEG_INF = float("-inf")

VMEM_LIMIT_BYTES = 56 * 1024 * 1024
DECODE_PAGES_PER_STEP = 8
SEL_BLOCKS_PER_STEP = 5

_PROJ_HEADS = dict(nq=(0, 8), kc=(8, 2), ks=(10, 2), kw=(12, 2), sb_q=(14, 8), sb_k=(22, 8), sb_v=(30, 8),
                   vc=(38, 2), vs=(40, 2), vw=(42, 2), ng=(44, 1))
_N_ROPE_HEADS = 14
_N_PROJ_HEADS = 45
_PROJ_TILE_HEADS = 9


def _cparams(*sem):
    return pltpu.CompilerParams(dimension_semantics=sem, vmem_limit_bytes=VMEM_LIMIT_BYTES)


def _adaln_kernel(c_ref, w_ref, b_ref, o_ref):
    c = c_ref[...]
    a = (c * jax.nn.sigmoid(c)).astype(BF16)
    o_ref[...] = jnp.dot(a, w_ref[...].astype(BF16), preferred_element_type=F32) + b_ref[...]


def adaln_mod(c, w_ada, b_ada, tn=1024):
    m, d = c.shape
    n = w_ada.shape[1]
    return pl.pallas_call(
        _adaln_kernel,
        grid=(n // tn,),
        in_specs=[pl.BlockSpec((m, d), lambda j: (0, 0)),
                  pl.BlockSpec((d, tn), lambda j: (0, j)),
                  pl.BlockSpec((1, tn), lambda j: (0, j))],
        out_specs=pl.BlockSpec((m, tn), lambda j: (0, j)),
        out_shape=jax.ShapeDtypeStruct((m, n), F32),
        compiler_params=_cparams("arbitrary"),
        name="adaln_mod",
    )(c, w_ada, b_ada.reshape(1, n))


def _modnorm_kernel(x_ref, g_ref, sh_ref, sc_ref, o_ref):
    x = x_ref[...]
    y = x * lax.rsqrt(jnp.mean(x * x, axis=-1, keepdims=True) + RMS_EPS)
    y = y * g_ref[...]
    o_ref[...] = (y * (1.0 + sc_ref[...]) + sh_ref[...]).astype(o_ref.dtype)


def modnorm(x, g, mod3, k_shift, k_scale, tm, tiles_per_batch):
    m, d = x.shape
    r = mod3.shape[1]
    return pl.pallas_call(
        _modnorm_kernel,
        grid=(m // tm,),
        in_specs=[pl.BlockSpec((tm, d), lambda i: (i, 0)),
                  pl.BlockSpec((1, d), lambda i: (0, 0)),
                  pl.BlockSpec((None, r, d), lambda i: (i // tiles_per_batch, 0, k_shift)),
                  pl.BlockSpec((None, r, d), lambda i: (i // tiles_per_batch, 0, k_scale))],
        out_specs=pl.BlockSpec((tm, d), lambda i: (i, 0)),
        out_shape=jax.ShapeDtypeStruct((m, d), BF16),
        compiler_params=_cparams("arbitrary"),
        name="modnorm",
    )(x, g.reshape(1, d), mod3, mod3)


def _inproj_kernel(h_ref, w_ref, cos_ref, sin_ref, o_ref):
    j = pl.program_id(1)
    acc = jnp.dot(h_ref[...], w_ref[...], preferred_element_type=F32)
    for hh in range(_PROJ_TILE_HEADS):
        a = acc[:, hh * HEAD_DIM:(hh + 1) * HEAD_DIM]
        head = j * _PROJ_TILE_HEADS + hh

        @pl.when(head < _N_ROPE_HEADS)
        def _():
            o_ref[:, hh * HEAD_DIM:(hh + 1) * HEAD_DIM] = (
                a * cos_ref[...] + pltpu.roll(a, HEAD_DIM // 2, 1) * sin_ref[...])

        @pl.when(head >= _N_ROPE_HEADS)
        def _():
            o_ref[:, hh * HEAD_DIM:(hh + 1) * HEAD_DIM] = a


def in_proj(hn, w_proj, cos, sin, tm):
    m, d = hn.shape
    n = w_proj.shape[1]
    tn = _PROJ_TILE_HEADS * HEAD_DIM
    npos = cos.shape[0] // tm
    return pl.pallas_call(
        _inproj_kernel,
        grid=(m // tm, n // tn),
        in_specs=[pl.BlockSpec((tm, d), lambda i, j: (i, 0)),
                  pl.BlockSpec((d, tn), lambda i, j: (0, j)),
                  pl.BlockSpec((tm, HEAD_DIM), lambda i, j: (i % npos, 0)),
                  pl.BlockSpec((tm, HEAD_DIM), lambda i, j: (i % npos, 0))],
        out_specs=pl.BlockSpec((tm, tn), lambda i, j: (i, j)),
        out_shape=jax.ShapeDtypeStruct((m, n), F32),
        compiler_params=_cparams("arbitrary", "arbitrary"),
        name="in_proj",
    )(hn, w_proj, cos, sin)


def _split_hi_lo(x):
    hi = x.astype(BF16)
    lo = (x - hi.astype(F32)).astype(BF16)
    return hi, lo


def _softplus_terms(z):
    t = jnp.log1p(jnp.exp(-jnp.abs(z)))
    return jnp.minimum(z, 0.0) - t, jnp.minimum(-z, 0.0) - t


def _sb_prompt_kernel(q_ref, k_ref, v_ref, o_ref, *, t):
    qi = pl.program_id(2)
    q = q_ref[...].astype(BF16)
    row = lax.broadcasted_iota(jnp.int32, (t, t), 0)
    col = lax.broadcasted_iota(jnp.int32, (t, t), 1)
    later = (row > col).astype(BF16)

    def body(step, carry):
        acc, c = carry
        k0 = pl.multiple_of((qi - step) * t, t)
        k = k_ref[pl.ds(k0, t), :].astype(BF16)
        v = v_ref[pl.ds(k0, t), :].astype(BF16)
        z = lax.dot_general(q, k, (((1,), (1,)), ((), ())), preferred_element_type=F32) * SCALE
        ls, lr = _softplus_terms(z)
        mask = jnp.logical_or(step > 0, col < row)
        lr = jnp.where(mask, lr, 0.0)
        hi, lo = _split_hi_lo(lr)
        tail = (jnp.dot(hi, later, preferred_element_type=F32)
                + jnp.dot(lo, later, preferred_element_type=F32))
        w = jnp.where(mask, jnp.exp(ls + tail + c), 0.0)
        acc = acc + jnp.dot(w.astype(BF16), v, preferred_element_type=F32)
        c = c + jnp.sum(lr, axis=1, keepdims=True)
        return acc, c

    acc, _ = lax.fori_loop(0, qi + 1, body,
                           (jnp.zeros((t, HEAD_DIM), F32), jnp.zeros((t, 1), F32)))
    o_ref[...] = acc.astype(o_ref.dtype)


def sb_prompt(proj, batch, seq, t=256):
    nq = seq // t
    q0, k0, v0 = _PROJ_HEADS["sb_q"][0], _PROJ_HEADS["sb_k"][0], _PROJ_HEADS["sb_v"][0]
    return pl.pallas_call(
        functools.partial(_sb_prompt_kernel, t=t),
        grid=(batch, SB_HEADS, nq),
        in_specs=[pl.BlockSpec((t, HEAD_DIM), lambda b, h, i: (b * nq + i, q0 + h)),
                  pl.BlockSpec((seq, HEAD_DIM), lambda b, h, i: (b, k0 + h)),
                  pl.BlockSpec((seq, HEAD_DIM), lambda b, h, i: (b, v0 + h))],
        out_specs=pl.BlockSpec((t, HEAD_DIM), lambda b, h, i: (b * nq + i, h)),
        out_shape=jax.ShapeDtypeStruct((batch * seq, SB_HEADS * HEAD_DIM), BF16),
        compiler_params=_cparams("arbitrary", "arbitrary", "arbitrary"),
        name="sb_prompt",
    )(proj, proj, proj)


def _masked_softmax(z, mask):
    z = jnp.where(mask, z, NEG_INF)
    m = jnp.max(z, axis=-1, keepdims=True)
    e = jnp.exp(z - m)
    return e / jnp.sum(e, axis=-1, keepdims=True)


def _top_blocks_mask(imp, cur, n_sel):
    rows, nb = imp.shape
    blk = lax.broadcasted_iota(jnp.int32, (rows, nb), 1)
    impm = jnp.where(blk == cur, jnp.inf, jnp.where(blk > cur, NEG_INF, imp))
    rank = jnp.zeros((rows, nb), jnp.int32)
    for i in range(nb):
        ci = impm[:, i:i + 1]
        ahead = jnp.logical_or(ci > impm, jnp.logical_and(ci == impm, blk > i))
        rank = rank + ahead.astype(jnp.int32)
    return jnp.logical_and(rank < n_sel, blk <= cur)


def _nsa_prompt_kernel(q_ref, kc_ref, vc_ref, ks_ref, vs_ref, kw_ref, vw_ref, ng_ref, o_ref,
                       kb_ref, vb_ref, *, tq, seq):
    g = pl.program_id(1)
    qi = pl.program_id(2)
    nb = seq // CMP_BLOCK
    hq = NSA_HPG * tq

    @pl.when(qi == 0)
    def _():
        kb_ref[...] = jnp.mean(kc_ref[...].reshape(nb, CMP_BLOCK, HEAD_DIM), axis=1)
        vb_ref[...] = jnp.mean(vc_ref[...].reshape(nb, CMP_BLOCK, HEAD_DIM), axis=1)

    s0 = qi * tq
    q4 = q_ref[...]
    qs = jnp.concatenate([q4[:, h * HEAD_DIM:(h + 1) * HEAD_DIM] for h in range(NSA_HPG)], axis=0).astype(BF16)
    qpos = s0 + lax.broadcasted_iota(jnp.int32, (hq, 1), 0) % tq
    qpos_t = s0 + lax.broadcasted_iota(jnp.int32, (tq, 1), 0)
    nt = (((1,), (1,)), ((), ()))

    z = lax.dot_general(qs, kb_ref[...].astype(BF16), nt, preferred_element_type=F32) * SCALE
    blk = lax.broadcasted_iota(jnp.int32, (hq, nb), 1)
    vis = (blk + 1) * CMP_BLOCK - 1 <= qpos
    zmax = jnp.max(jnp.where(vis, z, NEG_INF), axis=-1, keepdims=True)
    zmax = jnp.where(jnp.isfinite(zmax), zmax, 0.0)
    e = jnp.where(vis, jnp.exp(z - zmax), 0.0)
    p = e / jnp.maximum(jnp.sum(e, axis=-1, keepdims=True), jnp.finfo(F32).tiny)
    o_cmp = jnp.dot(p.astype(BF16), vb_ref[...].astype(BF16), preferred_element_type=F32)
    imp = p[0:tq]
    for h in range(1, NSA_HPG):
        imp = imp + p[h * tq:(h + 1) * tq]

    sel = _top_blocks_mask(imp, qpos_t // CMP_BLOCK, min(N_SEL, nb))
    expand = (lax.broadcasted_iota(jnp.int32, (nb, seq), 1) // CMP_BLOCK
              == lax.broadcasted_iota(jnp.int32, (nb, seq), 0)).astype(BF16)
    keymask = jnp.dot(sel.astype(BF16), expand, preferred_element_type=F32) > 0.5
    keymask = jnp.logical_and(keymask, lax.broadcasted_iota(jnp.int32, (tq, seq), 1) <= qpos_t)
    keymask = jnp.concatenate([keymask.astype(F32)] * NSA_HPG, axis=0) > 0.5
    zs = lax.dot_general(qs, ks_ref[...].astype(BF16), nt, preferred_element_type=F32) * SCALE
    ps = _masked_softmax(zs, keymask)
    o_sel = jnp.dot(ps.astype(BF16), vs_ref[...].astype(BF16), preferred_element_type=F32)

    span = WINDOW + tq
    w0 = pl.multiple_of(jnp.maximum(s0 - WINDOW, 0), tq)
    kwin = kw_ref[pl.ds(w0, span), :].astype(BF16)
    vwin = vw_ref[pl.ds(w0, span), :].astype(BF16)
    rel = qpos - (w0 + lax.broadcasted_iota(jnp.int32, (hq, span), 1))
    zw = lax.dot_general(qs, kwin, nt, preferred_element_type=F32) * SCALE
    pw = _masked_softmax(zw, jnp.logical_and(rel >= 0, rel < WINDOW))
    o_win = jnp.dot(pw.astype(BF16), vwin, preferred_element_type=F32)

    gate = jax.nn.sigmoid(ng_ref[...])
    lane = lax.broadcasted_iota(jnp.int32, (tq, HEAD_DIM), 1)
    outs = []
    for h in range(NSA_HPG):
        base = (g * NSA_HPG + h) * 3
        gs = [jnp.sum(jnp.where(lane == base + br, gate, 0.0), axis=-1, keepdims=True) for br in range(3)]
        rows = slice(h * tq, (h + 1) * tq)
        outs.append(gs[0] * o_cmp[rows] + gs[1] * o_sel[rows] + gs[2] * o_win[rows])
    o_ref[...] = jnp.concatenate(outs, axis=1).astype(o_ref.dtype)


def nsa_prompt(proj, batch, seq, tq=128):
    nq = seq // tq
    nb = seq // CMP_BLOCK
    hd = _PROJ_HEADS

    def kv_spec(name):
        first = hd[name][0]
        return pl.BlockSpec((seq, HEAD_DIM), lambda b, g, i: (b, first + g))

    return pl.pallas_call(
        functools.partial(_nsa_prompt_kernel, tq=tq, seq=seq),
        grid=(batch, NSA_GROUPS, nq),
        in_specs=[pl.BlockSpec((tq, NSA_HPG * HEAD_DIM), lambda b, g, i: (b * nq + i, g)),
                  kv_spec("kc"), kv_spec("vc"), kv_spec("ks"), kv_spec("vs"), kv_spec("kw"), kv_spec("vw"),
                  pl.BlockSpec((tq, HEAD_DIM), lambda b, g, i: (b * nq + i, hd["ng"][0]))],
        out_specs=pl.BlockSpec((tq, NSA_HPG * HEAD_DIM), lambda b, g, i: (b * nq + i, g)),
        out_shape=jax.ShapeDtypeStruct((batch * seq, NSA_HEADS * HEAD_DIM), BF16),
        scratch_shapes=[pltpu.VMEM((nb, HEAD_DIM), F32), pltpu.VMEM((nb, HEAD_DIM), F32)],
        compiler_params=_cparams("arbitrary", "arbitrary", "arbitrary"),
        name="nsa_prompt",
    )(proj, proj, proj, proj, proj, proj, proj, proj)


def _merge_kernel(h_ref, osb_ref, onsa_ref, wgs_ref, wgn_ref, wus_ref, wun_ref, o_ref):
    h = h_ref[...]
    gs = jax.nn.sigmoid(jnp.dot(h, wgs_ref[...], preferred_element_type=F32))
    gn = jax.nn.sigmoid(jnp.dot(h, wgn_ref[...], preferred_element_type=F32))
    us = jnp.dot(osb_ref[...], wus_ref[...], preferred_element_type=F32)
    un = jnp.dot(onsa_ref[...], wun_ref[...], preferred_element_type=F32)
    o_ref[...] = (gs * us + gn * un).astype(o_ref.dtype)


def merge_branches(hn, o_sb, o_nsa, wg_sb, wg_nsa, wu_sb, wu_nsa, tm, tn=512):
    m, d = hn.shape
    w = o_sb.shape[1]
    return pl.pallas_call(
        _merge_kernel,
        grid=(m // tm, d // tn),
        in_specs=[pl.BlockSpec((tm, d), lambda i, j: (i, 0)),
                  pl.BlockSpec((tm, w), lambda i, j: (i, 0)),
                  pl.BlockSpec((tm, w), lambda i, j: (i, 0)),
                  pl.BlockSpec((d, tn), lambda i, j: (0, j)),
                  pl.BlockSpec((d, tn), lambda i, j: (0, j)),
                  pl.BlockSpec((w, tn), lambda i, j: (0, j)),
                  pl.BlockSpec((w, tn), lambda i, j: (0, j))],
        out_specs=pl.BlockSpec((tm, tn), lambda i, j: (i, j)),
        out_shape=jax.ShapeDtypeStruct((m, d), BF16),
        compiler_params=_cparams("arbitrary", "arbitrary"),
        name="merge_branches",
    )(hn, o_sb, o_nsa, wg_sb, wg_nsa, wu_sb, wu_nsa)


def _oproj_kernel(a_ref, w_ref, x_ref, gt_ref, o_ref):
    o_ref[...] = x_ref[...] + gt_ref[...] * jnp.dot(a_ref[...], w_ref[...], preferred_element_type=F32)


def oproj_residual(a, w, x, mod3, k_gate, tm, tiles_per_batch, tn=1024):
    m, d = x.shape
    r = mod3.shape[1]
    nj = d // tn
    return pl.pallas_call(
        _oproj_kernel,
        grid=(m // tm, nj),
        in_specs=[pl.BlockSpec((tm, a.shape[1]), lambda i, j: (i, 0)),
                  pl.BlockSpec((a.shape[1], tn), lambda i, j: (0, j)),
                  pl.BlockSpec((tm, tn), lambda i, j: (i, j)),
                  pl.BlockSpec((None, r, tn), lambda i, j: (i // tiles_per_batch, 0, k_gate * nj + j))],
        out_specs=pl.BlockSpec((tm, tn), lambda i, j: (i, j)),
        out_shape=jax.ShapeDtypeStruct((m, d), F32),
        compiler_params=_cparams("arbitrary", "arbitrary"),
        name="oproj_residual",
    )(a, w, x, mod3)


def _matmul_kernel(a_ref, w_ref, o_ref):
    o_ref[...] = jnp.dot(a_ref[...], w_ref[...], preferred_element_type=F32)


def matmul(a, w, tm, tn=1024):
    m, k = a.shape
    n = w.shape[1]
    return pl.pallas_call(
        _matmul_kernel,
        grid=(m // tm, n // tn),
        in_specs=[pl.BlockSpec((tm, k), lambda i, j: (i, 0)),
                  pl.BlockSpec((k, tn), lambda i, j: (0, j))],
        out_specs=pl.BlockSpec((tm, tn), lambda i, j: (i, j)),
        out_shape=jax.ShapeDtypeStruct((m, n), F32),
        compiler_params=_cparams("arbitrary", "arbitrary"),
        name="matmul",
    )(a, w)


def _extract_top(x, k):
    n = x.shape[0]
    ridx = lax.broadcasted_iota(jnp.int32, x.shape, 0)
    picked = jnp.zeros(x.shape, jnp.bool_)
    vals = []
    for _ in range(k):
        m = jnp.max(x, axis=0, keepdims=True)
        first = jnp.min(jnp.where(x == m, ridx, n), axis=0, keepdims=True)
        hit = ridx == first
        vals.append(m)
        picked = jnp.logical_or(picked, hit)
        x = jnp.where(hit, NEG_INF, x)
    return jnp.concatenate(vals, axis=0), picked


def _peer_route_kernel(q_ref, keys_ref, thr0_ref, s1_ref, a0_ref, b1_ref):
    nt = (((1,), (1,)), ((), ()))
    for h in range(PEER_HEADS):
        s, vals, picked = [], [], []
        for c in range(2):
            col = (h * 2 + c) * HEAD_DIM
            qhc = q_ref[:, col:col + HEAD_DIM].astype(BF16)
            sc = lax.dot_general(keys_ref[h * 2 + c].astype(BF16), qhc, nt, preferred_element_type=F32)
            v, pk = _extract_top(sc, PEER_TOPK)
            s.append(sc)
            vals.append(v)
            picked.append(pk)
        cand = jnp.concatenate([vals[0][a:a + 1] + vals[1] for a in range(PEER_TOPK)], axis=0)
        fin, _ = _extract_top(cand, PEER_TOPK + 1)
        z = jnp.sum(jnp.exp(fin[:PEER_TOPK] - fin[0:1]), axis=0, keepdims=True)
        cut = 0.5 * (fin[PEER_TOPK - 1:PEER_TOPK] + fin[PEER_TOPK:PEER_TOPK + 1])
        thr0_ref[h] = jnp.where(picked[0], cut - s[0], jnp.inf)
        s1_ref[h] = jnp.where(picked[1], s[1], NEG_INF)
        a0_ref[h] = jnp.where(picked[0], jnp.exp(s[0] - vals[0][0:1]), 0.0) / z
        b1_ref[h] = jnp.where(picked[1], jnp.exp(s[1] - vals[1][0:1]), 0.0)


def peer_route(qp, keys, tm):
    m = qp.shape[0]
    big = jax.ShapeDtypeStruct((PEER_HEADS, N_KEYS, m), F32)
    bspec = pl.BlockSpec((PEER_HEADS, N_KEYS, tm), lambda i: (0, 0, i))
    return pl.pallas_call(
        _peer_route_kernel,
        grid=(m // tm,),
        in_specs=[pl.BlockSpec((tm, qp.shape[1]), lambda i: (i, 0)),
                  pl.BlockSpec((2 * PEER_HEADS, N_KEYS, HEAD_DIM), lambda i: (0, 0, 0))],
        out_specs=[bspec, bspec, bspec, bspec],
        out_shape=[big, big, big, big],
        compiler_params=_cparams("arbitrary"),
        name="peer_route",
    )(qp, keys.reshape(2 * PEER_HEADS, N_KEYS, HEAD_DIM))


def _peer_dense_kernel(h_ref, u_ref, v_ref, thr0_ref, s1_ref, a0_ref, b1_ref, o_ref, gated_ref, wt_ref, *, te):
    e = pl.program_id(1)
    last = pl.num_programs(1) - 2
    slot = e % 2

    @pl.when(e == 0)
    def _():
        o_ref[...] = jnp.zeros_like(o_ref)
        gated_ref[1] = jnp.zeros(gated_ref.shape[1:], gated_ref.dtype)

    i0 = jnp.minimum(e, last) * (te // N_KEYS)
    for ii in range(te // N_KEYS):
        rows = slice(ii * N_KEYS, (ii + 1) * N_KEYS)
        thr_rows = [thr0_ref[hh, pl.ds(i0 + ii, 1), :] for hh in range(PEER_HEADS)]
        a0_rows = [a0_ref[hh, pl.ds(i0 + ii, 1), :] for hh in range(PEER_HEADS)]
        for tb in range(h_ref.shape[0] // HEAD_DIM):
            lanes = slice(tb * HEAD_DIM, (tb + 1) * HEAD_DIM)
            wt = None
            for hh in range(PEER_HEADS):
                keep = s1_ref[hh, :, lanes] >= thr_rows[hh][:, lanes]
                term = jnp.where(keep, a0_rows[hh][:, lanes] * b1_ref[hh, :, lanes], 0.0)
                wt = term if wt is None else wt + term
            wt_ref[rows, lanes] = wt

    act_in = lax.dot_general(u_ref[...], h_ref[...], (((1,), (1,)), ((), ())), preferred_element_type=F32)
    o_ref[...] += lax.dot_general(gated_ref[1 - slot], v_ref[...], (((0,), (0,)), ((), ())),
                                  preferred_element_type=F32)
    act = 0.5 * act_in * (1.0 + lax.erf(act_in * (2.0 ** -0.5)))
    gated_ref[slot] = (wt_ref[...] * act).astype(BF16)


def peer_dense(hn, u, v, route, tm, te=512):
    m, d = hn.shape
    n_tiles = u.shape[0] // te
    rspec = pl.BlockSpec((PEER_HEADS, N_KEYS, tm), lambda i, e: (0, 0, i))
    return pl.pallas_call(
        functools.partial(_peer_dense_kernel, te=te),
        grid=(m // tm, n_tiles + 1),
        in_specs=[pl.BlockSpec((tm, d), lambda i, e: (i, 0)),
                  pl.BlockSpec((te, d), lambda i, e: (jnp.minimum(e, n_tiles - 1), 0)),
                  pl.BlockSpec((te, d), lambda i, e: (jnp.maximum(e - 1, 0), 0)),
                  rspec, rspec, rspec, rspec],
        out_specs=pl.BlockSpec((tm, d), lambda i, e: (i, 0)),
        out_shape=jax.ShapeDtypeStruct((m, d), F32),
        scratch_shapes=[pltpu.VMEM((2, te, tm), BF16), pltpu.VMEM((te, tm), F32)],
        compiler_params=_cparams("arbitrary", "arbitrary"),
        name="peer_dense",
    )(hn, u, v, *route)


def _final_kernel(x_ref, y_ref, gt_ref, g_ref, o_ref):
    h = x_ref[...] + gt_ref[...] * y_ref[...]
    o_ref[...] = h * lax.rsqrt(jnp.mean(h * h, axis=-1, keepdims=True) + RMS_EPS) * g_ref[...]


def final_norm(x, y, mod3, k_gate, g, tm, tiles_per_batch):
    m, d = x.shape
    r = mod3.shape[1]
    return pl.pallas_call(
        _final_kernel,
        grid=(m // tm,),
        in_specs=[pl.BlockSpec((tm, d), lambda i: (i, 0)),
                  pl.BlockSpec((tm, d), lambda i: (i, 0)),
                  pl.BlockSpec((None, r, d), lambda i: (i // tiles_per_batch, 0, k_gate)),
                  pl.BlockSpec((1, d), lambda i: (0, 0))],
        out_specs=pl.BlockSpec((tm, d), lambda i: (i, 0)),
        out_shape=jax.ShapeDtypeStruct((m, d), F32),
        compiler_params=_cparams("arbitrary"),
        name="final_norm",
    )(x, y, mod3, g.reshape(1, d))


def _token_lane_onehot(n):
    tok = lax.broadcasted_iota(jnp.int32, (n, 1, n), 0)
    lane = lax.broadcasted_iota(jnp.int32, (n, 1, n), 2)
    return tok == lane


def _sb_decode_kernel(pt_ref, q_ref, *refs, pages):
    del pt_ref
    k_refs, v_refs = refs[:pages], refs[pages:2 * pages]
    o_ref, acc_ref, c_ref = refs[2 * pages:]
    i = pl.program_id(1)

    @pl.when(i == 0)
    def _():
        acc_ref[...] = jnp.zeros_like(acc_ref)
        c_ref[...] = jnp.zeros_like(c_ref)

    n = k_refs[0].shape[0]
    onehot = _token_lane_onehot(n)
    row = lax.broadcasted_iota(jnp.int32, (n, n), 0)
    col = lax.broadcasted_iota(jnp.int32, (n, n), 1)
    later = (row > col).astype(BF16)
    q = q_ref[...][None]
    nh, hd = q_ref.shape
    spread = (lax.broadcasted_iota(jnp.int32, (n, n * nh), 1) // nh
              == lax.broadcasted_iota(jnp.int32, (n, n * nh), 0)).astype(BF16)
    own_head = (lax.broadcasted_iota(jnp.int32, (nh, n * nh), 1) % nh
                == lax.broadcasted_iota(jnp.int32, (nh, n * nh), 0))
    per_page = []
    for j in range(pages):
        zs = jnp.sum(k_refs[j][...] * q, axis=-1, keepdims=True) * SCALE
        z = jnp.sum(jnp.where(onehot, zs, 0.0), axis=0)
        ls, lr = _softplus_terms(z)
        hi, lo = _split_hi_lo(lr)
        tail = jnp.dot(hi, later, preferred_element_type=F32) + jnp.dot(lo, later, preferred_element_type=F32)
        per_page.append((ls + tail, jnp.sum(lr, axis=1, keepdims=True)))
    c = c_ref[...]
    acc = acc_ref[...]
    for j in range(pages):
        base, rowsum = per_page[j]
        w = jnp.exp(base + c).astype(BF16)
        wrows = jnp.where(own_head, jnp.dot(w, spread, preferred_element_type=F32), 0.0).astype(BF16)
        vp = v_refs[j][...].reshape(n * nh, hd).astype(BF16)
        acc = acc + jnp.dot(wrows, vp, preferred_element_type=F32)
        c = c + rowsum
    acc_ref[...] = acc
    c_ref[...] = c

    @pl.when(i == pl.num_programs(1) - 1)
    def _():
        o_ref[...] = acc


def sb_decode(q, cache5, page_table, pages=DECODE_PAGES_PER_STEP):
    bsz, nh, hd = q.shape
    n_pages = page_table.shape[1]
    page = cache5.shape[1]

    def kv_spec(which, j):
        return pl.BlockSpec((None, page, None, nh, hd),
                            lambda b, i, pt: (pt[b, n_pages - 1 - (i * pages + j)], 0, which, 0, 0))

    return pl.pallas_call(
        functools.partial(_sb_decode_kernel, pages=pages),
        grid_spec=pltpu.PrefetchScalarGridSpec(
            num_scalar_prefetch=1,
            grid=(bsz, n_pages // pages),
            in_specs=([pl.BlockSpec((None, nh, hd), lambda b, i, pt: (b, 0, 0))]
                      + [kv_spec(0, j) for j in range(pages)] + [kv_spec(1, j) for j in range(pages)]),
            out_specs=pl.BlockSpec((None, nh, hd), lambda b, i, pt: (b, 0, 0)),
            scratch_shapes=[pltpu.VMEM((nh, hd), F32), pltpu.VMEM((nh, hd), F32)]),
        out_shape=jax.ShapeDtypeStruct((bsz, nh, hd), F32),
        compiler_params=_cparams("arbitrary", "arbitrary"),
        name="sb_decode",
    )(page_table, q, *([cache5] * (2 * pages)))


def _nsa_compress_kernel(pt_ref, *refs, pages):
    del pt_ref
    x_refs, o_ref = refs[:pages], refs[pages]
    p = pl.program_id(1)
    for j in range(pages):
        x = x_refs[j][...]
        per_page = x.shape[0] // CMP_BLOCK
        for part in range(per_page):
            s = jnp.sum(x[part * CMP_BLOCK:(part + 1) * CMP_BLOCK], axis=0) * (1.0 / CMP_BLOCK)
            blk = (p * pages + j) * per_page + part
            for r in range(2 * NSA_GROUPS):
                o_ref[r, pl.ds(blk, 1), :] = s[r:r + 1, :]


def nsa_compress(cache4, page_table, pages=DECODE_PAGES_PER_STEP):
    bsz, n_pages = page_table.shape
    page, rows, hd = cache4.shape[1:]
    nb = n_pages * page // CMP_BLOCK

    def page_spec(j):
        return pl.BlockSpec((None, page, rows, hd), lambda b, p, pt: (pt[b, p * pages + j], 0, 0, 0))

    return pl.pallas_call(
        functools.partial(_nsa_compress_kernel, pages=pages),
        grid_spec=pltpu.PrefetchScalarGridSpec(
            num_scalar_prefetch=1,
            grid=(bsz, n_pages // pages),
            in_specs=[page_spec(j) for j in range(pages)],
            out_specs=pl.BlockSpec((None, 2 * NSA_GROUPS, nb, hd), lambda b, p, pt: (b, 0, 0, 0))),
        out_shape=jax.ShapeDtypeStruct((bsz, 2 * NSA_GROUPS, nb, hd), F32),
        compiler_params=_cparams("arbitrary", "arbitrary"),
        name="nsa_compress",
    )(page_table, *([cache4] * pages))


def _by_group(rowg, fn):
    return jnp.where(rowg == 0, fn(0), fn(1))


def _nsa_decode_cmp_kernel(q_ref, kvb_ref, win_ref, nw_ref, ocmp_ref, owin_ref, idx_ref, *, qpos, win_len):
    nt = (((1,), (1,)), ((), ()))
    q = q_ref[...]
    qb = q.astype(BF16)
    rowg = lax.broadcasted_iota(jnp.int32, (NSA_HEADS, 1), 0) // NSA_HPG
    nb = kvb_ref.shape[1]

    z = _by_group(rowg, lambda g: lax.dot_general(qb, kvb_ref[g].astype(BF16), nt, preferred_element_type=F32)) * SCALE
    blk = lax.broadcasted_iota(jnp.int32, (NSA_HEADS, nb), 1)
    vis = (blk + 1) * CMP_BLOCK - 1 <= qpos
    zmax = jnp.max(jnp.where(vis, z, NEG_INF), axis=-1, keepdims=True)
    zmax = jnp.where(jnp.isfinite(zmax), zmax, 0.0)
    e = jnp.where(vis, jnp.exp(z - zmax), 0.0)
    p = e / jnp.maximum(jnp.sum(e, axis=-1, keepdims=True), jnp.finfo(F32).tiny)
    pb = p.astype(BF16)
    ocmp_ref[...] = _by_group(rowg, lambda g: jnp.dot(pb, kvb_ref[NSA_GROUPS + g].astype(BF16),
                                                     preferred_element_type=F32))

    eye = (lax.broadcasted_iota(jnp.int32, (nb, nb), 0) == lax.broadcasted_iota(jnp.int32, (nb, nb), 1))
    before = lax.broadcasted_iota(jnp.int32, (nb, nb), 0) < lax.broadcasted_iota(jnp.int32, (nb, nb), 1)
    slot = lax.broadcasted_iota(jnp.int32, (N_SEL, nb), 0)
    lane = lax.broadcasted_iota(jnp.int32, (N_SEL, nb), 1).astype(F32)
    for g in range(NSA_GROUPS):
        imp = jnp.sum(p[g * NSA_HPG:(g + 1) * NSA_HPG], axis=0, keepdims=True)
        impcol = jnp.sum(jnp.where(eye, imp, 0.0), axis=1, keepdims=True)
        ahead = jnp.logical_or(impcol > imp, jnp.logical_and(impcol == imp, before))
        rank = jnp.sum(ahead.astype(F32), axis=0, keepdims=True)
        chosen = jnp.sum(jnp.where(rank == slot.astype(F32), lane, 0.0), axis=1, keepdims=True)
        idx_ref[g] = jnp.broadcast_to(chosen, (N_SEL, nb)).astype(jnp.int32)

    nw = nw_ref[...]
    zw = _by_group(rowg, lambda g: lax.dot_general(
        qb, win_ref[pl.ds(g, win_len, stride=2 * NSA_GROUPS), :].astype(BF16), nt, preferred_element_type=F32)) * SCALE
    kpos = qpos - win_len + lax.broadcasted_iota(jnp.int32, (NSA_HEADS, win_len), 1)
    rel = qpos - kpos
    wmask = jnp.logical_and(jnp.logical_and(rel >= 0, rel < WINDOW), kpos >= 0)
    k_new = _by_group(rowg, lambda g: nw[g:g + 1])
    v_new = _by_group(rowg, lambda g: nw[NSA_GROUPS + g:NSA_GROUPS + g + 1])
    z_new = jnp.sum(q * k_new, axis=-1, keepdims=True) * SCALE
    m = jnp.maximum(jnp.max(jnp.where(wmask, zw, NEG_INF), axis=-1, keepdims=True), z_new)
    ew = jnp.where(wmask, jnp.exp(zw - m), 0.0)
    en = jnp.exp(z_new - m)
    ewb = ew.astype(BF16)
    num = _by_group(rowg, lambda g: jnp.dot(
        ewb, win_ref[pl.ds(NSA_GROUPS + g, win_len, stride=2 * NSA_GROUPS), :].astype(BF16),
        preferred_element_type=F32))
    owin_ref[...] = (num + en * v_new) / (jnp.sum(ew, axis=-1, keepdims=True) + en)


def nsa_decode_cmp(q, kvb, win2d, new_win, qpos):
    bsz, nh, hd = q.shape
    nb = kvb.shape[2]
    win_len = win2d.shape[1] // (2 * NSA_GROUPS)
    vec = jax.ShapeDtypeStruct((bsz, nh, hd), F32)
    vspec = pl.BlockSpec((None, nh, hd), lambda b: (b, 0, 0))
    return pl.pallas_call(
        functools.partial(_nsa_decode_cmp_kernel, qpos=qpos, win_len=win_len),
        grid=(bsz,),
        in_specs=[vspec,
                  pl.BlockSpec((None,) + kvb.shape[1:], lambda b: (b, 0, 0, 0)),
                  pl.BlockSpec((None,) + win2d.shape[1:], lambda b: (b, 0, 0)),
                  pl.BlockSpec((None,) + new_win.shape[1:], lambda b: (b, 0, 0))],
        out_specs=[vspec, vspec, pl.BlockSpec((None, NSA_GROUPS, N_SEL, nb), lambda b: (b, 0, 0, 0))],
        out_shape=[vec, vec, jax.ShapeDtypeStruct((bsz, NSA_GROUPS, N_SEL, nb), jnp.int32)],
        compiler_params=_cparams("arbitrary"),
        name="nsa_decode_cmp",
    )(q, kvb, win2d, new_win)


def _nsa_decode_sel_kernel(pt_ref, si_ref, q_ref, nn_ref, ocmp_ref, owin_ref, ng_ref, *refs, per_step):
    del pt_ref, si_ref
    xs = (refs[:per_step], refs[per_step:2 * per_step])
    o_ref, m_ref, l_ref, acc_ref = refs[2 * per_step:]
    nt = (((1,), (1,)), ((), ()))
    s = pl.program_id(1)
    q = q_ref[...]
    qb = q.astype(BF16)
    rowg = lax.broadcasted_iota(jnp.int32, (NSA_HEADS, 1), 0) // NSA_HPG
    rows = 4 * NSA_GROUPS

    @pl.when(s == 0)
    def _():
        nn = nn_ref[...]
        k_new = _by_group(rowg, lambda g: nn[2 * NSA_GROUPS + g:2 * NSA_GROUPS + g + 1])
        v_new = _by_group(rowg, lambda g: nn[3 * NSA_GROUPS + g:3 * NSA_GROUPS + g + 1])
        m_ref[...] = jnp.broadcast_to(jnp.sum(q * k_new, axis=-1, keepdims=True) * SCALE, m_ref.shape)
        l_ref[...] = jnp.ones_like(l_ref)
        acc_ref[...] = v_new

    zs = [_by_group(rowg, lambda g, k=k: lax.dot_general(
        qb, xs[g][k][pl.ds(2 * NSA_GROUPS + g, CMP_BLOCK, stride=rows), :].astype(BF16), nt,
        preferred_element_type=F32)) * SCALE for k in range(per_step)]
    m_old = m_ref[...]
    m_new = m_old
    for z in zs:
        m_new = jnp.maximum(m_new, jnp.max(z, axis=-1, keepdims=True))
    alpha = jnp.exp(m_old - m_new)
    den = alpha * l_ref[...]
    num = alpha * acc_ref[...]
    for k in range(per_step):
        p = jnp.exp(zs[k] - m_new[:, 0:1])
        pb = p.astype(BF16)
        den = den + jnp.sum(p, axis=-1, keepdims=True)
        num = num + _by_group(rowg, lambda g, k=k: jnp.dot(
            pb, xs[g][k][pl.ds(3 * NSA_GROUPS + g, CMP_BLOCK, stride=rows), :].astype(BF16),
            preferred_element_type=F32))
    m_ref[...] = m_new
    l_ref[...] = den
    acc_ref[...] = num

    @pl.when(s == pl.num_programs(1) - 1)
    def _():
        gate = jax.nn.sigmoid(ng_ref[...])
        o_ref[...] = gate[0] * ocmp_ref[...] + gate[1] * (num / den) + gate[2] * owin_ref[...]


def nsa_decode_sel(q, cache_half, page_table, sel_idx, new_nsa, o_cmp, o_win, ng3, per_step=SEL_BLOCKS_PER_STEP):
    bsz, nh, hd = q.shape
    half_rows = cache_half.shape[2]
    blocks_per_page = cache_half.shape[1]
    n_steps = (N_SEL - 1) // per_step
    assert n_steps * per_step == N_SEL - 1
    vspec = pl.BlockSpec((None, nh, hd), lambda b, s, pt, si: (b, 0, 0))

    def blk_spec(g, k):
        def index(b, s, pt, si):
            blk = si[b, g * N_SEL + s * per_step + k]
            return (pt[b, blk // blocks_per_page], blk % blocks_per_page, 0, 0)
        return pl.BlockSpec((None, None, half_rows, hd), index)

    return pl.pallas_call(
        functools.partial(_nsa_decode_sel_kernel, per_step=per_step),
        grid_spec=pltpu.PrefetchScalarGridSpec(
            num_scalar_prefetch=2,
            grid=(bsz, n_steps),
            in_specs=([vspec, vspec, vspec, vspec,
                       pl.BlockSpec((None, 3, nh, hd), lambda b, s, pt, si: (b, 0, 0, 0))]
                      + [blk_spec(g, k) for g in range(NSA_GROUPS) for k in range(per_step)]),
            out_specs=vspec,
            scratch_shapes=[pltpu.VMEM((nh, hd), F32)] * 3),
        out_shape=jax.ShapeDtypeStruct((bsz, nh, hd), F32),
        compiler_params=_cparams("arbitrary", "arbitrary"),
        name="nsa_decode_sel",
    )(page_table, sel_idx, q, new_nsa, o_cmp, o_win, ng3, *([cache_half] * (NSA_GROUPS * per_step)))


def _rope_tables(pos):
    half = HEAD_DIM // 2
    inv_freq = ROPE_THETA ** (-jnp.arange(half, dtype=F32) / half)
    ang = pos.astype(F32)[:, None] * inv_freq[None, :]
    cos, sin = jnp.cos(ang), jnp.sin(ang)
    return jnp.concatenate([cos, cos], axis=-1), jnp.concatenate([-sin, sin], axis=-1)


def _head_cols(proj, name):
    first, n = _PROJ_HEADS[name]
    return proj[:, first * HEAD_DIM:(first + n) * HEAD_DIM]


def kernel(x_prompt, x_sample, cache_sb_kv, cache_nsa_kv, cache_win_kv, page_table, c_prompt, c_sample, norm1_g, norm2_g, w_ada, b_ada, w_in, w_up_sb, w_up_nsa, w_o, peer_wq, peer_keys, peer_u, peer_v, final_g):
    batch, seq, d = x_prompt.shape
    dec_batch = x_sample.shape[0]
    n_pool, page = cache_sb_kv.shape[1:3]
    past = page_table.shape[1] * page
    layer = 0

    w = w_in[layer]
    off = dict(sb_q=0, sb_k=1024, sb_v=2048, nq=3072, kc=4096, vc=4352, ks=4608, vs=4864, kw=5120, vw=5376)
    order = ("nq", "kc", "ks", "kw", "sb_q", "sb_k", "sb_v", "vc", "vs", "vw")
    n_gate = 3 * NSA_HEADS
    pieces = [w[:, off[n]:off[n] + _PROJ_HEADS[n][1] * HEAD_DIM] for n in order]
    pieces.append(jnp.pad(w[:, 5632:5632 + n_gate], ((0, 0), (0, HEAD_DIM - n_gate))))
    w_proj = jnp.concatenate(pieces, axis=1).astype(BF16)
    wg_sb = w[:, 5632 + n_gate:5632 + n_gate + d].astype(BF16)
    wg_nsa = w[:, 5632 + n_gate + d:].astype(BF16)
    wu_sb, wu_nsa = w_up_sb[layer].astype(BF16), w_up_nsa[layer].astype(BF16)
    wo, wq = w_o[layer].astype(BF16), peer_wq[layer].astype(BF16)
    u, v = peer_u[layer].astype(BF16), peer_v[layer].astype(BF16)
    keys = peer_keys[layer]

    n_mod = batch + dec_batch
    c_all = jnp.pad(jnp.concatenate([c_prompt, c_sample], axis=0), ((0, (-n_mod) % 8), (0, 0)))
    mod = adaln_mod(c_all, w_ada[layer], b_ada[layer])
    mod_p = mod[:batch].reshape(batch, 1, N_ADA * d)
    mod_s = mod[batch:n_mod].reshape(1, dec_batch, N_ADA * d)

    tm = 1024
    tpb = seq // tm
    xp = x_prompt.reshape(batch * seq, d)
    cos_p, sin_p = _rope_tables(jnp.arange(seq, dtype=jnp.int32))
    hn = modnorm(xp, norm1_g[layer], mod_p, 0, 1, 512, seq // 512)
    proj = in_proj(hn, w_proj, cos_p, sin_p, tm)
    o_sb = sb_prompt(proj, batch, seq)
    o_nsa = nsa_prompt(proj, batch, seq)
    merged = merge_branches(hn, o_sb, o_nsa, wg_sb, wg_nsa, wu_sb, wu_nsa, tm)
    h1 = oproj_residual(merged, wo, xp, mod_p, 2, tm, tpb)
    hn2 = modnorm(h1, norm2_g[layer], mod_p, 3, 4, 512, seq // 512)
    qp = matmul(hn2, wq, tm)
    y_p = peer_dense(hn2, u, v, peer_route(qp, keys, 256), 512)
    y_prompt = final_norm(h1, y_p, mod_p, 5, final_g, 512, seq // 512)

    proj3 = proj.reshape(batch, seq, _N_PROJ_HEADS * HEAD_DIM)
    sb_kv_prompt = jnp.concatenate([_head_cols(proj, "sb_k"), _head_cols(proj, "sb_v")], axis=1)
    nsa_kv_prompt = jnp.concatenate([_head_cols(proj, n) for n in ("kc", "vc", "ks", "vs")], axis=1)
    win_len_p = min(WINDOW, seq)
    tail = proj3[:, seq - win_len_p:].reshape(batch * win_len_p, -1)
    win_kv_prompt = jnp.concatenate([_head_cols(tail, "kw"), _head_cols(tail, "vw")], axis=1)

    xs = x_sample.reshape(dec_batch, d)
    cos_s, sin_s = _rope_tables(jnp.full((dec_batch,), past, jnp.int32))
    hn_s = modnorm(xs, norm1_g[layer], mod_s, 0, 1, dec_batch, 1)
    proj_s = in_proj(hn_s, w_proj, cos_s, sin_s, dec_batch)
    new_sb = jnp.concatenate([_head_cols(proj_s, "sb_k"), _head_cols(proj_s, "sb_v")], axis=1)
    new_nsa = jnp.concatenate([_head_cols(proj_s, n) for n in ("kc", "vc", "ks", "vs")], axis=1)
    new_win = jnp.concatenate([_head_cols(proj_s, "kw"), _head_cols(proj_s, "vw")], axis=1)

    cache_sb5 = cache_sb_kv[layer].reshape(n_pool, page, 2, SB_HEADS, HEAD_DIM)
    q_sb = _head_cols(proj_s, "sb_q").reshape(dec_batch, SB_HEADS, HEAD_DIM)
    o_sb_s = sb_decode(q_sb, cache_sb5, page_table).reshape(dec_batch, SB_HEADS * HEAD_DIM).astype(BF16)

    rows = 4 * NSA_GROUPS
    cache_nsa4 = cache_nsa_kv[layer].reshape(n_pool, page, rows, HEAD_DIM)
    cache_half = cache_nsa_kv[layer].reshape(n_pool, page // CMP_BLOCK, CMP_BLOCK * rows, HEAD_DIM)
    win_buf = cache_win_kv.shape[2]
    win2d = cache_win_kv[layer].reshape(dec_batch, win_buf * 2 * NSA_GROUPS, HEAD_DIM)
    q_nsa = _head_cols(proj_s, "nq").reshape(dec_batch, NSA_HEADS, HEAD_DIM)
    kvb = nsa_compress(cache_nsa4, page_table)
    o_cmp, o_win, sel = nsa_decode_cmp(q_nsa, kvb, win2d, new_win.reshape(dec_batch, 2 * NSA_GROUPS, HEAD_DIM), past)
    sel_idx = sel[:, :, :, 0].reshape(dec_batch, NSA_GROUPS * N_SEL)
    ng3 = _head_cols(proj_s, "ng")[:, :n_gate].reshape(dec_batch, NSA_HEADS, 3).transpose(0, 2, 1)
    ng3 = jnp.broadcast_to(ng3[..., None], (dec_batch, 3, NSA_HEADS, HEAD_DIM))
    o_nsa_s = nsa_decode_sel(q_nsa, cache_half, page_table, sel_idx, new_nsa.reshape(dec_batch, rows, HEAD_DIM),
                             o_cmp, o_win, ng3).reshape(dec_batch, NSA_HEADS * HEAD_DIM).astype(BF16)


    merged_s = merge_branches(hn_s, o_sb_s, o_nsa_s, wg_sb, wg_nsa, wu_sb, wu_nsa, dec_batch)
    h1_s = oproj_residual(merged_s, wo, xs, mod_s, 2, dec_batch, 1)
    hn2_s = modnorm(h1_s, norm2_g[layer], mod_s, 3, 4, dec_batch, 1)
    lane_pad = (-dec_batch) % HEAD_DIM
    hn2_sp = jnp.pad(hn2_s, ((0, lane_pad), (0, 0)))
    tm_s = dec_batch + lane_pad
    qp_s = matmul(hn2_sp, wq, tm_s)
    y_s = peer_dense(hn2_sp, u, v, peer_route(qp_s, keys, tm_s), tm_s)
    y_sample = final_norm(h1_s, y_s[:dec_batch], mod_s, 5, final_g, dec_batch, 1)

    win_kv_sample = jnp.concatenate(
        [cache_win_kv[layer][:, 1:], new_win.reshape(dec_batch, 1, 2, NSA_GROUPS, HEAD_DIM)], axis=1)
    return (y_prompt.reshape(batch, seq, d),
            y_sample.reshape(dec_batch, 1, d),
            sb_kv_prompt.reshape(1, batch, seq, 2, SB_HEADS, HEAD_DIM),
            new_sb.reshape(1, dec_batch, 1, 2, SB_HEADS, HEAD_DIM),
            nsa_kv_prompt.reshape(1, batch, seq, 4, NSA_GROUPS, HEAD_DIM),
            new_nsa.reshape(1, dec_batch, 1, 4, NSA_GROUPS, HEAD_DIM),
            win_kv_prompt.reshape(1, batch, win_len_p, 2, NSA_GROUPS, HEAD_DIM),
            win_kv_sample[None])
```
